```python
import math
import jax, jax.numpy as jnp
from jax import lax
import numpy as np

D_MODEL = 1024
BATCH = 4
SEQ = 8192
DEPTH = 1

HEAD_DIM = 64
N_HEADS_A = 8
N_KV_A = 2
GROUP_A = N_HEADS_A // N_KV_A
N_HEADS_B = 8
WINDOW = 128
WIN_BLOCK = 128
T5_BUCKETS = 32
T5_MAX_DIST = 128
GRID_W = 64
NA_ROWS = 8
NA_COLS = 16
PEER_HEADS = 8
PEER_KEYS = 128
N_EXPERTS = PEER_KEYS * PEER_KEYS
PEER_QDIM = 256
PEER_HALF = PEER_QDIM // 2
PEER_TOPK = 16
PEER_CHUNK = 128
EPS = 1e-6
NEG = -1e30

W_QA = N_HEADS_A * HEAD_DIM
W_KVA = N_KV_A * HEAD_DIM
W_B = N_HEADS_B * HEAD_DIM
D_MIX = W_QA + W_B
D_IN = W_QA + 2 * W_KVA + 3 * W_B
IN_SPLITS = [W_QA, W_QA + W_KVA, W_QA + 2 * W_KVA, W_QA + 2 * W_KVA + W_B, W_QA + 2 * W_KVA + 2 * W_B]

kernel_name = "hymba_window_natten_peer_block"


def rms_norm(x, g):
    xf = x.astype(jnp.float32)
    y = xf * lax.rsqrt(jnp.mean(xf * xf, axis=-1, keepdims=True) + EPS)
    return (y * g.astype(jnp.float32)).astype(x.dtype)


def modulate(h, shift, scale):
    return h * (1.0 + scale[:, None, :]) + shift[:, None, :]


def t5_bucket(rel):
    half = T5_BUCKETS // 2
    max_exact = half // 2
    ret = jnp.where(rel > 0, half, 0)
    n = jnp.abs(rel)
    nf = jnp.maximum(n, 1).astype(jnp.float32)
    large = max_exact + (jnp.log(nf / max_exact) / math.log(T5_MAX_DIST / max_exact)
                         * (half - max_exact)).astype(jnp.int32)
    large = jnp.minimum(large, half - 1)
    return ret + jnp.where(n < max_exact, n, large)


def windowed_gqa(q, k, v, sink, t5_table):
    B, S = q.shape[0], q.shape[1]
    nb = S // WIN_BLOCK
    span = WIN_BLOCK + 2 * WINDOW
    pad = ((0, 0), (WINDOW, WINDOW), (0, 0), (0, 0))
    k_pad = jnp.pad(k, pad)
    v_pad = jnp.pad(v, pad)
    i = jnp.arange(WIN_BLOCK)[:, None]
    j = jnp.arange(span)[None, :]
    rel = j - WINDOW - i
    band = jnp.abs(rel) <= WINDOW
    bias = t5_table[t5_bucket(rel)].transpose(2, 0, 1).astype(jnp.float32)
    scale = HEAD_DIM ** -0.5
    sink_f = sink.astype(jnp.float32)

    def block(b):
        qb = lax.dynamic_slice_in_dim(q, b * WIN_BLOCK, WIN_BLOCK, axis=1)
        qb = qb.reshape(B, WIN_BLOCK, N_KV_A, GROUP_A, HEAD_DIM)
        kb = lax.dynamic_slice_in_dim(k_pad, b * WIN_BLOCK, span, axis=1)
        vb = lax.dynamic_slice_in_dim(v_pad, b * WIN_BLOCK, span, axis=1)
        s = jnp.einsum('bqgrd,bkgd->bgrqk', qb, kb).astype(jnp.float32)
        s = s.reshape(B, N_HEADS_A, WIN_BLOCK, span) * scale + bias
        kpos = b * WIN_BLOCK - WINDOW + jnp.arange(span)
        valid = band & ((kpos >= 0) & (kpos < S))[None, :]
        s = jnp.where(valid, s, NEG)
        sink_col = jnp.broadcast_to(sink_f[None, :, None, None], (B, N_HEADS_A, WIN_BLOCK, 1))
        p = jax.nn.softmax(jnp.concatenate([s, sink_col], axis=-1), axis=-1)[..., :span]
        p = p.astype(v.dtype).reshape(B, N_KV_A, GROUP_A, WIN_BLOCK, span)
        o = jnp.einsum('bgrqk,bkgd->bqgrd', p, vb)
        return o.reshape(B, WIN_BLOCK, N_HEADS_A * HEAD_DIM)

    out = lax.map(block, jnp.arange(nb))
    return out.transpose(1, 0, 2, 3).reshape(B, S, N_HEADS_A * HEAD_DIM)


def neighbourhood_attention(q, k, v, rpb):
    B, S = q.shape[0], q.shape[1]
    rows = S // GRID_W
    kr = min(NA_ROWS, rows)
    q_g = q.reshape(B, rows, GRID_W, N_HEADS_B, HEAD_DIM)
    k_g = k.reshape(B, rows, GRID_W, N_HEADS_B, HEAD_DIM)
    v_g = v.reshape(B, rows, GRID_W, N_HEADS_B, HEAD_DIM)
    cols = jnp.arange(GRID_W)
    col_start = jnp.clip(cols - NA_COLS // 2, 0, GRID_W - NA_COLS)
    col_idx = col_start[:, None] + jnp.arange(NA_COLS)[None, :]
    col_off = col_idx - cols[:, None] + (NA_COLS - 1)
    scale = HEAD_DIM ** -0.5
    rpb_f = rpb.astype(jnp.float32)

    def row_block(r):
        rs = jnp.clip(r - kr // 2, 0, rows - kr)
        q_r = lax.dynamic_index_in_dim(q_g, r, axis=1, keepdims=False)
        k_band = lax.dynamic_slice_in_dim(k_g, rs, kr, axis=1)
        v_band = lax.dynamic_slice_in_dim(v_g, rs, kr, axis=1)
        k_win = k_band[:, :, col_idx]
        v_win = v_band[:, :, col_idx]
        s = jnp.einsum('bchd,bkcwhd->bhckw', q_r, k_win).astype(jnp.float32) * scale
        row_off = rs + jnp.arange(kr) - r + (NA_ROWS - 1)
        bias = rpb_f[:, row_off[None, :, None], col_off[:, None, :]]
        s = (s + bias[None]).reshape(B, N_HEADS_B, GRID_W, kr * NA_COLS)
        p = jax.nn.softmax(s, axis=-1).reshape(B, N_HEADS_B, GRID_W, kr, NA_COLS).astype(v.dtype)
        return jnp.einsum('bhckw,bkcwhd->bchd', p, v_win)

    out = lax.map(row_block, jnp.arange(rows))
    return out.transpose(1, 0, 2, 3, 4).reshape(B, S, N_HEADS_B * HEAD_DIM)


def peer(h, w_query, sub_keys, u_experts, v_experts):
    B, S, D = h.shape
    tokens = h.reshape(B * S // PEER_CHUNK, PEER_CHUNK, D)

    def chunk(xc):
        qh = (xc @ w_query).reshape(PEER_CHUNK, PEER_HEADS, PEER_QDIM)
        s1 = jnp.einsum('thd,hnd->thn', qh[..., :PEER_HALF], sub_keys[:, 0]).astype(jnp.float32)
        s2 = jnp.einsum('thd,hnd->thn', qh[..., PEER_HALF:], sub_keys[:, 1]).astype(jnp.float32)
        v1, i1 = lax.top_k(s1, PEER_TOPK)
        v2, i2 = lax.top_k(s2, PEER_TOPK)
        cand = (v1[..., :, None] + v2[..., None, :]).reshape(PEER_CHUNK, PEER_HEADS, PEER_TOPK * PEER_TOPK)
        vs, ci = lax.top_k(cand, PEER_TOPK)
        e1 = jnp.take_along_axis(i1, ci // PEER_TOPK, axis=-1)
        e2 = jnp.take_along_axis(i2, ci % PEER_TOPK, axis=-1)
        idx = e1 * PEER_KEYS + e2
        g = jax.nn.softmax(vs, axis=-1)
        u_sel = u_experts[idx]
        v_sel = v_experts[idx]
        z = jnp.einsum('td,thkd->thk', xc, u_sel)
        a = (jax.nn.gelu(z.astype(jnp.float32)) * g).astype(xc.dtype)
        return jnp.einsum('thk,thkd->td', a, v_sel)

    out = lax.map(chunk, tokens)
    return out.reshape(B, S, D)


def setup_inputs(seed: int = 0) -> dict:
    key = jax.random.key(seed)
    ks = jax.random.split(key, 20)
    f32 = jnp.float32
    D = D_MODEL
    nrm = lambda k, shape, s: jax.random.normal(k, shape, f32) * s
    return {
        "x": nrm(ks[0], (BATCH, SEQ, D), 1.0),
        "c": nrm(ks[1], (BATCH, D), 1.0),
        "w_ada": nrm(ks[2], (DEPTH, D, 6 * D), 0.5 * D ** -0.5),
        "b_ada": nrm(ks[3], (DEPTH, 6 * D), 0.02),
        "norm1_g": 1.0 + nrm(ks[4], (DEPTH, D), 0.02),
        "w_in": nrm(ks[5], (DEPTH, D, D_IN), D ** -0.5),
        "sink_a": nrm(ks[6], (DEPTH, N_HEADS_A), 0.5),
        "t5_table": nrm(ks[7], (T5_BUCKETS, N_HEADS_A), 0.3),
        "rpb_b": nrm(ks[8], (DEPTH, N_HEADS_B, 2 * NA_ROWS - 1, 2 * NA_COLS - 1), 0.3),
        "out_norm_a": 1.0 + nrm(ks[9], (DEPTH, W_QA), 0.02),
        "out_norm_b": 1.0 + nrm(ks[10], (DEPTH, W_B), 0.02),
        "w_out": nrm(ks[11], (DEPTH, D_MIX, D), D_MIX ** -0.5),
        "norm2_g": 1.0 + nrm(ks[12], (DEPTH, D), 0.02),
        "w_query": nrm(ks[13], (DEPTH, D, PEER_HEADS * PEER_QDIM), D ** -0.5),
        "sub_keys": nrm(ks[14], (DEPTH, PEER_HEADS, 2, PEER_KEYS, PEER_HALF), PEER_HALF ** -0.5),
        "u_experts": nrm(ks[15], (DEPTH, N_EXPERTS, D), D ** -0.5),
        "v_experts": nrm(ks[16], (DEPTH, N_EXPERTS, D), 1.0),
        "final_g": 1.0 + nrm(ks[17], (D,), 0.02),
    }


def reference(x, c, w_ada, b_ada, norm1_g, w_in, sink_a, t5_table, rpb_b, out_norm_a, out_norm_b,
              w_out, norm2_g, w_query, sub_keys, u_experts, v_experts, final_g):
    B, S, D = x.shape
    c_act = jax.nn.silu(c)
    for layer in range(DEPTH):
        ada = c_act @ w_ada[layer] + b_ada[layer]
        shift1, scale1, gate1, shift2, scale2, gate2 = jnp.split(ada, 6, axis=-1)

        h = modulate(rms_norm(x, norm1_g[layer]), shift1, scale1)
        proj = h @ w_in[layer]
        qa, ka, va, qb, kb, vb = jnp.split(proj, IN_SPLITS, axis=-1)
        o_a = windowed_gqa(qa.reshape(B, S, N_HEADS_A, HEAD_DIM),
                           ka.reshape(B, S, N_KV_A, HEAD_DIM),
                           va.reshape(B, S, N_KV_A, HEAD_DIM),
                           sink_a[layer], t5_table)
        o_b = neighbourhood_attention(qb.reshape(B, S, N_HEADS_B, HEAD_DIM),
                                      kb.reshape(B, S, N_HEADS_B, HEAD_DIM),
                                      vb.reshape(B, S, N_HEADS_B, HEAD_DIM),
                                      rpb_b[layer])
        mixed = jnp.concatenate([rms_norm(o_a, out_norm_a[layer]),
                                 rms_norm(o_b, out_norm_b[layer])], axis=-1)
        x = x + gate1[:, None, :] * (mixed @ w_out[layer])

        h2 = modulate(rms_norm(x, norm2_g[layer]), shift2, scale2)
        x = x + gate2[:, None, :] * peer(h2, w_query[layer], sub_keys[layer],
                                          u_experts[layer], v_experts[layer])
    return rms_norm(x, final_g)
```

```python
import functools
import math

import numpy as np
import jax
import jax.numpy as jnp
from jax import lax
from jax.experimental import pallas as pl
from jax.experimental.pallas import tpu as pltpu

F32 = jnp.float32
BF16 = jnp.bfloat16

HEAD_DIM = 64
N_HEADS_A = 8
N_KV_A = 2
N_HEADS_B = 8
WINDOW = 128
WIN_BLOCK = 128
T5_BUCKETS = 32
T5_MAX_DIST = 128
GRID_W = 64
NA_ROWS = 8
NA_COLS = 16
PEER_HEADS = 8
PEER_KEYS = 128
PEER_QDIM = 256
PEER_HALF = 128
PEER_TOPK = 16
N_PICKS = PEER_HEADS * PEER_TOPK
EPS = 1e-6
NEG = -1e30

W_QA = N_HEADS_A * HEAD_DIM
W_KVA = N_KV_A * HEAD_DIM
W_B = N_HEADS_B * HEAD_DIM

LANES = 128
D_SPLIT = 2
VMEM_LIMIT = 60 * 1024 * 1024


def _cparams(sem):
    return pltpu.CompilerParams(dimension_semantics=sem, vmem_limit_bytes=VMEM_LIMIT)


def _ada_kernel(c_ref, w_ref, b_ref, o_ref):
    c = c_ref[...]
    act = c * jax.nn.sigmoid(c)
    o_ref[...] = jnp.dot(act, w_ref[...], preferred_element_type=F32,
                         precision=lax.Precision.HIGHEST) + b_ref[...]


def _ada(c, w_ada, b_ada):
    B, D = c.shape
    n_out = w_ada.shape[1]
    rows = 8
    c_pad = jnp.zeros((rows, D), F32).at[:B].set(c)
    out = pl.pallas_call(
        _ada_kernel,
        grid=(n_out // D,),
        in_specs=[pl.BlockSpec((rows, D), lambda j: (0, 0)),
                  pl.BlockSpec((D, D), lambda j: (0, j)),
                  pl.BlockSpec((1, D), lambda j: (0, j))],
        out_specs=pl.BlockSpec((rows, D), lambda j: (0, j)),
        out_shape=jax.ShapeDtypeStruct((rows, n_out), F32),
        compiler_params=_cparams(("arbitrary",)),
        name="ada",
    )(c_pad, w_ada, b_ada.reshape(1, n_out))
    return out[:B].reshape(B, n_out // D, D)


def _rms(x, g):
    return x * lax.rsqrt(jnp.mean(x * x, axis=-1, keepdims=True) + EPS) * g


def _in_proj_kernel(x_ref, mod_ref, g_ref, wq_ref, wk_ref, wv_ref, wqb_ref, wkb_ref, wvb_ref,
                    qa_ref, ka_ref, va_ref, qb_ref, kb_ref, vb_ref):
    x = x_ref[0]
    h = _rms(x, g_ref[...]) * (1.0 + mod_ref[0, 1:2, :]) + mod_ref[0, 0:1, :]
    hb = h.astype(BF16)
    scale = HEAD_DIM ** -0.5
    qa_ref[0] = (jnp.dot(hb, wq_ref[...], preferred_element_type=F32) * scale).astype(BF16)
    ka_ref[0] = jnp.dot(hb, wk_ref[...], preferred_element_type=F32).astype(BF16)
    va_ref[0] = jnp.dot(hb, wv_ref[...], preferred_element_type=F32).astype(BF16)
    qb_ref[0] = (jnp.dot(hb, wqb_ref[...], preferred_element_type=F32) * scale).astype(BF16)
    kb_ref[0] = jnp.dot(hb, wkb_ref[...], preferred_element_type=F32).astype(BF16)
    vb_ref[0] = jnp.dot(hb, wvb_ref[...], preferred_element_type=F32).astype(BF16)


def _dup_kv_cols(w):
    a, b = w[:, :HEAD_DIM], w[:, HEAD_DIM:]
    return jnp.concatenate([a, a, b, b], axis=1)


def _in_proj(x, mods, g1, w_in, tm):
    B, S, D = x.shape
    wb = w_in.astype(BF16)
    o = 0
    wq = wb[:, o:o + W_QA]; o += W_QA
    wk = _dup_kv_cols(wb[:, o:o + W_KVA]); o += W_KVA
    wv = _dup_kv_cols(wb[:, o:o + W_KVA]); o += W_KVA
    wqb = wb[:, o:o + W_B]; o += W_B
    wkb = wb[:, o:o + W_B]; o += W_B
    wvb = wb[:, o:o + W_B]
    full = lambda a: pl.BlockSpec(a.shape, lambda b, i: (0, 0))
    tok = lambda w: pl.BlockSpec((1, tm, w), lambda b, i: (b, i, 0))
    widths = (W_QA, 2 * W_KVA, 2 * W_KVA, W_B, W_B, W_B)
    return pl.pallas_call(
        _in_proj_kernel,
        grid=(B, S // tm),
        in_specs=[tok(D), pl.BlockSpec((1, 6, D), lambda b, i: (b, 0, 0)), full(g1),
                  full(wq), full(wk), full(wv), full(wqb), full(wkb), full(wvb)],
        out_specs=[tok(w) for w in widths],
        out_shape=[jax.ShapeDtypeStruct((B, S, w), BF16) for w in widths],
        compiler_params=_cparams(("parallel", "parallel")),
        name="in_proj",
    )(x, mods, g1, wq, wk, wv, wqb, wkb, wvb)


def _t5_bucket_np(rel):
    half = T5_BUCKETS // 2
    max_exact = half // 2
    ret = np.where(rel > 0, half, 0)
    n = np.abs(rel)
    nf = np.maximum(n, 1).astype(np.float32)
    large = max_exact + (np.log(nf / np.float32(max_exact)) / np.float32(math.log(T5_MAX_DIST / max_exact))
                         * np.float32(half - max_exact)).astype(np.int32)
    large = np.minimum(large, half - 1)
    return ret + np.where(n < max_exact, n, large)


def _softmax_rows(s, extra=None):
    m = jnp.max(s, axis=-1, keepdims=True)
    if extra is not None:
        m = jnp.maximum(m, extra)
    e = jnp.exp(s - m)
    den = jnp.sum(e, axis=-1, keepdims=True)
    if extra is not None:
        den = den + jnp.exp(extra - m)
    return e / den


def _win_attn_kernel(sink_ref, q_ref, kp_ref, kc_ref, kn_ref, vp_ref, vc_ref, vn_ref, bias_ref, o_ref, *, seq):
    i = pl.program_id(1)
    q = q_ref[0]
    k = jnp.concatenate([kp_ref[0], kc_ref[0], kn_ref[0]], axis=0)
    v = jnp.concatenate([vp_ref[0], vc_ref[0], vn_ref[0]], axis=0)
    span = k.shape[0]
    kpos = lax.broadcasted_iota(jnp.int32, (WIN_BLOCK, span), 1) + (i - 1) * WIN_BLOCK
    valid = (kpos >= 0) & (kpos < seq)
    low = lax.broadcasted_iota(jnp.int32, (WIN_BLOCK, LANES), 1) < HEAD_DIM
    zero = jnp.zeros((WIN_BLOCK, LANES), BF16)
    for pair in range(N_HEADS_A // 2):
        qp = q[:, pair * LANES:(pair + 1) * LANES]
        outs = []
        for par in range(2):
            h = 2 * pair + par
            grp = h // (N_HEADS_A // N_KV_A)
            kg = k[:, grp * LANES:(grp + 1) * LANES]
            vg = v[:, grp * LANES:(grp + 1) * LANES]
            qh = jnp.where(low if par == 0 else jnp.logical_not(low), qp, zero)
            s = lax.dot_general(qh, kg, (((1,), (1,)), ((), ())), preferred_element_type=F32)
            s = jnp.where(valid, s + bias_ref[h], NEG)
            p = _softmax_rows(s, extra=sink_ref[h]).astype(BF16)
            outs.append(jnp.dot(p, vg, preferred_element_type=F32))
        o_ref[0, :, pair * LANES:(pair + 1) * LANES] = jnp.where(low, outs[0], outs[1])


def _win_attn(qa, ka, va, sink, t5_table):
    B, S, _ = qa.shape
    nb = S // WIN_BLOCK
    span = WIN_BLOCK + 2 * WINDOW
    rel = np.arange(span)[None, :] - WINDOW - np.arange(WIN_BLOCK)[:, None]
    band = np.abs(rel) <= WINDOW
    bias = t5_table[_t5_bucket_np(rel)].transpose(2, 0, 1).astype(F32)
    bias = jnp.where(band[None], bias, NEG)
    qspec = pl.BlockSpec((1, WIN_BLOCK, W_QA), lambda b, i: (b, i, 0))
    kv = lambda off: pl.BlockSpec((1, WIN_BLOCK, 2 * W_KVA),
                                  lambda b, i: (b, jnp.clip(i + off, 0, nb - 1), 0))
    return pl.pallas_call(
        functools.partial(_win_attn_kernel, seq=S),
        grid=(B, nb),
        in_specs=[pl.BlockSpec(memory_space=pltpu.SMEM), qspec,
                  kv(-1), kv(0), kv(1), kv(-1), kv(0), kv(1),
                  pl.BlockSpec(bias.shape, lambda b, i: (0, 0, 0))],
        out_specs=pl.BlockSpec((1, WIN_BLOCK, W_QA), lambda b, i: (b, i, 0)),
        out_shape=jax.ShapeDtypeStruct((B, S, W_QA), F32),
        compiler_params=_cparams(("parallel", "parallel")),
        name="win_attn",
    )(sink.astype(F32), qa, ka, ka, ka, va, va, va, bias)


def _nbr_attn_kernel(*refs):
    q_ref = refs[0]
    k_refs = refs[1:1 + NA_ROWS]
    v_refs = refs[1 + NA_ROWS:1 + 2 * NA_ROWS]
    bias_ref = refs[1 + 2 * NA_ROWS]
    o_ref = refs[2 + 2 * NA_ROWS]
    q = q_ref[0]
    k = jnp.concatenate([r[0] for r in k_refs], axis=0)
    v = jnp.concatenate([r[0] for r in v_refs], axis=0)
    low = lax.broadcasted_iota(jnp.int32, (GRID_W, LANES), 1) < HEAD_DIM
    zero = jnp.zeros((GRID_W, LANES), BF16)
    for pair in range(N_HEADS_B // 2):
        sl = slice(pair * LANES, (pair + 1) * LANES)
        qp, kp, vp = q[:, sl], k[:, sl], v[:, sl]
        outs = []
        for par in range(2):
            h = 2 * pair + par
            qh = jnp.where(low if par == 0 else jnp.logical_not(low), qp, zero)
            s = lax.dot_general(qh, kp, (((1,), (1,)), ((), ())), preferred_element_type=F32)
            p = _softmax_rows(s + bias_ref[0, h]).astype(BF16)
            outs.append(jnp.dot(p, vp, preferred_element_type=F32))
        o_ref[0, :, sl] = jnp.where(low, outs[0], outs[1])


def _nbr_bias(rpb):
    cols = np.arange(GRID_W)
    col_start = np.clip(cols - NA_COLS // 2, 0, GRID_W - NA_COLS)
    kc = np.arange(GRID_W)
    inside = (kc[None, :] >= col_start[:, None]) & (kc[None, :] < col_start[:, None] + NA_COLS)
    col_off = np.clip(kc[None, :] - cols[:, None] + (NA_COLS - 1), 0, 2 * NA_COLS - 2)
    variants = []
    for d in range(NA_ROWS):
        row_off = np.arange(NA_ROWS) - d + (NA_ROWS - 1)
        b = rpb[:, row_off[None, :, None], col_off[:, None, :]].astype(F32)
        b = jnp.where(inside[None, :, None, :], b, NEG)
        variants.append(b.reshape(N_HEADS_B, GRID_W, NA_ROWS * GRID_W))
    return jnp.stack(variants)


def _nbr_attn(qb, kb, vb, rpb):
    B, S, _ = qb.shape
    rows = S // GRID_W
    assert rows >= NA_ROWS
    bias = _nbr_bias(rpb)
    band_start = lambda r: jnp.clip(r - NA_ROWS // 2, 0, rows - NA_ROWS)
    qspec = pl.BlockSpec((1, GRID_W, W_B), lambda b, r: (b, r, 0))
    kvspec = lambda j: pl.BlockSpec((1, GRID_W, W_B), lambda b, r: (b, band_start(r) + j, 0))
    return pl.pallas_call(
        _nbr_attn_kernel,
        grid=(B, rows),
        in_specs=[qspec] + [kvspec(j) for j in range(NA_ROWS)] * 2
                 + [pl.BlockSpec((1,) + bias.shape[1:], lambda b, r: (r - band_start(r), 0, 0, 0))],
        out_specs=pl.BlockSpec((1, GRID_W, W_B), lambda b, r: (b, r, 0)),
        out_shape=jax.ShapeDtypeStruct((B, S, W_B), F32),
        compiler_params=_cparams(("parallel", "parallel")),
        name="nbr_attn",
    )(qb, *([kb] * NA_ROWS), *([vb] * NA_ROWS), bias)


def _out_proj_kernel(oa_ref, ob_ref, x_ref, mod_ref, ga_ref, gb_ref, wa_ref, wb_ref, g2_ref, x1_ref, h2_ref):
    na = _rms(oa_ref[0], ga_ref[...]).astype(BF16)
    nb = _rms(ob_ref[0], gb_ref[...]).astype(BF16)
    y = (jnp.dot(na, wa_ref[...], preferred_element_type=F32)
         + jnp.dot(nb, wb_ref[...], preferred_element_type=F32))
    x1 = x_ref[0] + mod_ref[0, 2:3, :] * y
    x1_ref[0] = x1
    h2_ref[0] = _rms(x1, g2_ref[...]) * (1.0 + mod_ref[0, 4:5, :]) + mod_ref[0, 3:4, :]


def _out_proj(o_a, o_b, x, mods, ga, gb, w_out, g2, tm):
    B, S, D = x.shape
    wb16 = w_out.astype(BF16)
    wa, wb = wb16[:W_QA], wb16[W_QA:]
    full = lambda a: pl.BlockSpec(a.shape, lambda b, i: (0, 0))
    tok = lambda w: pl.BlockSpec((1, tm, w), lambda b, i: (b, i, 0))
    return pl.pallas_call(
        _out_proj_kernel,
        grid=(B, S // tm),
        in_specs=[tok(W_QA), tok(W_B), tok(D), pl.BlockSpec((1, 6, D), lambda b, i: (b, 0, 0)),
                  full(ga), full(gb), full(wa), full(wb), full(g2)],
        out_specs=[tok(D), tok(D)],
        out_shape=[jax.ShapeDtypeStruct((B, S, D), F32)] * 2,
        compiler_params=_cparams(("parallel", "parallel")),
        name="out_proj",
    )(o_a, o_b, x, mods, ga, gb, wa, wb, g2)


def _topk_rows(s, k):
    n_rows = s.shape[0]
    row = lax.broadcasted_iota(jnp.int32, s.shape, 0)
    vals, idxs = [], []
    for _ in range(k):
        m = jnp.max(s, axis=0, keepdims=True)
        am = jnp.min(jnp.where(s == m, row, n_rows), axis=0, keepdims=True)
        vals.append(m)
        idxs.append(am)
        s = jnp.where(row == am, -jnp.inf, s)
    return jnp.concatenate(vals, axis=0), jnp.concatenate(idxs, axis=0)


def _select_rows(table, sel):
    out = jnp.zeros(sel.shape, table.dtype)
    for r in range(table.shape[0]):
        out = jnp.where(sel == r, table[r:r + 1, :], out)
    return out


def _route_kernel(h_ref, wq_ref, keys_ref, idx_ref, gate_ref):
    hb = h_ref[...].astype(BF16)
    idx_rows, gate_rows = [], []
    for h in range(PEER_HEADS):
        tops = []
        for side in range(2):
            c0 = h * PEER_QDIM + side * PEER_HALF
            qh = jnp.dot(hb, wq_ref[:, c0:c0 + PEER_HALF], preferred_element_type=F32).astype(BF16)
            s = lax.dot_general(keys_ref[h, side], qh, (((1,), (1,)), ((), ())),
                                preferred_element_type=F32)
            tops.append(_topk_rows(s, PEER_TOPK))
        (v1, i1), (v2, i2) = tops
        cand = jnp.concatenate([v1[a:a + 1, :] + v2 for a in range(PEER_TOPK)], axis=0)
        vs, ci = _topk_rows(cand, PEER_TOPK)
        e1 = _select_rows(i1, ci // PEER_TOPK)
        e2 = _select_rows(i2, ci % PEER_TOPK)
        idx_rows.append(e1 * PEER_KEYS + e2)
        ex = jnp.exp(vs - jnp.max(vs, axis=0, keepdims=True))
        gate_rows.append(ex / jnp.sum(ex, axis=0, keepdims=True))
    idx_ref[...] = jnp.concatenate(idx_rows, axis=0).T
    gate_ref[...] = jnp.concatenate(gate_rows, axis=0).T


def _route(h2, w_query, sub_keys, tm):
    T, D = h2.shape
    wq = w_query.astype(BF16)
    keys = sub_keys.astype(BF16)
    return pl.pallas_call(
        _route_kernel,
        grid=(T // tm,),
        in_specs=[pl.BlockSpec((tm, D), lambda i: (i, 0)),
                  pl.BlockSpec(wq.shape, lambda i: (0, 0)),
                  pl.BlockSpec(keys.shape, lambda i: (0, 0, 0, 0))],
        out_specs=[pl.BlockSpec((tm, N_PICKS), lambda i: (i, 0))] * 2,
        out_shape=[jax.ShapeDtypeStruct((T, N_PICKS), jnp.int32),
                   jax.ShapeDtypeStruct((T, N_PICKS), F32)],
        compiler_params=_cparams(("parallel",)),
        name="route",
    )(h2, wq, keys)


def _eye(n):
    return lax.broadcasted_iota(jnp.int32, (n, n), 0) == lax.broadcasted_iota(jnp.int32, (n, n), 1)


def _slab_sums8(slabs):
    p = slabs
    merged = None
    row4 = lax.broadcasted_iota(jnp.int32, (8, LANES), 0) % 4
    for r, (first, second) in enumerate(((p[3], p[7]), (p[4], p[0]), (p[5], p[1]), (p[6], p[2]))):
        f = jnp.concatenate([first, second], axis=0)
        f = f + pltpu.roll(f, 2, axis=0)
        f = f + pltpu.roll(f, 1, axis=0)
        if r:
            f = pltpu.roll(f, r, axis=0)
        merged = f if merged is None else jnp.where(row4 == (3 + r) % 4, f, merged)
    return merged


def _peer_u_kernel(idx_ref, x_ref, tbl_ref, z_ref):
    tb = x_ref.shape[0]
    eye = _eye(N_PICKS)

    def token(t, carry):
        x = x_ref[t]
        prods = [tbl_ref[idx_ref[t, k]] * x for k in range(N_PICKS)]
        sums = jnp.concatenate([_slab_sums8(prods[8 * g:8 * g + 8]) for g in range(N_PICKS // 8)], axis=0)
        col = jnp.sum(sums, axis=1, keepdims=True)
        z_ref[pl.ds(t, 1), :] = jnp.sum(jnp.where(eye, col, 0.0), axis=0, keepdims=True)
        return carry

    lax.fori_loop(0, tb, token, 0)


def _gelu_tanh(z):
    return 0.5 * z * (1.0 + jnp.tanh(math.sqrt(2.0 / math.pi) * (z + 0.044715 * (z * z * z))))


def _peer_v_kernel(idx_ref, z_ref, gate_ref, tbl_ref, o_ref, a_scr, arep_scr):
    tb = o_ref.shape[0]
    z = z_ref[0]
    for part in range(1, z_ref.shape[0]):
        z = z + z_ref[part]
    a_scr[...] = _gelu_tanh(z) * gate_ref[...]
    eye = _eye(N_PICKS)
    n_acc = 4

    def token(t, carry):
        a_row = a_scr[pl.ds(t, 1), :]
        col = jnp.sum(jnp.where(eye, a_row, 0.0), axis=1, keepdims=True)
        arep_scr[...] = jnp.broadcast_to(col, (N_PICKS, LANES))
        accs = [None] * n_acc
        for k in range(N_PICKS):
            term = arep_scr[pl.ds(k, 1), :] * tbl_ref[idx_ref[t, k]]
            accs[k % n_acc] = term if accs[k % n_acc] is None else accs[k % n_acc] + term
        o_ref[t] = (accs[0] + accs[1]) + (accs[2] + accs[3])
        return carry

    lax.fori_loop(0, tb, token, 0)


def _expert_table_spec(n_experts, rows):
    return pl.BlockSpec((n_experts, None, rows, LANES), lambda h, i: (0, h, 0, 0),
                        pipeline_mode=pl.Buffered(1))


def _peer(h2, idx, gate, u_experts, v_experts, tb):
    T, D = h2.shape
    n_experts = u_experts.shape[0]
    rows = D // (D_SPLIT * LANES)
    slab = lambda a: a.reshape(a.shape[0], D_SPLIT, rows, LANES)
    idx_spec = pl.BlockSpec((tb, N_PICKS), lambda h, i: (i, 0), memory_space=pltpu.SMEM)
    tok_slab = pl.BlockSpec((tb, None, rows, LANES), lambda h, i: (i, h, 0, 0))
    z = pl.pallas_call(
        _peer_u_kernel,
        grid=(D_SPLIT, T // tb),
        in_specs=[idx_spec, tok_slab, _expert_table_spec(n_experts, rows)],
        out_specs=pl.BlockSpec((None, tb, N_PICKS), lambda h, i: (h, i, 0)),
        out_shape=jax.ShapeDtypeStruct((D_SPLIT, T, N_PICKS), F32),
        compiler_params=_cparams(("arbitrary", "arbitrary")),
        name="peer_u",
    )(idx, slab(h2), slab(u_experts))
    out = pl.pallas_call(
        _peer_v_kernel,
        grid=(D_SPLIT, T // tb),
        in_specs=[idx_spec,
                  pl.BlockSpec((D_SPLIT, tb, N_PICKS), lambda h, i: (0, i, 0)),
                  pl.BlockSpec((tb, N_PICKS), lambda h, i: (i, 0)),
                  _expert_table_spec(n_experts, rows)],
        out_specs=tok_slab,
        out_shape=jax.ShapeDtypeStruct((T, D_SPLIT, rows, LANES), F32),
        scratch_shapes=[pltpu.VMEM((tb, N_PICKS), F32), pltpu.VMEM((N_PICKS, LANES), F32)],
        compiler_params=_cparams(("arbitrary", "arbitrary")),
        name="peer_v",
    )(idx, z, gate, slab(v_experts))
    return out.reshape(T, D)


def _final_kernel(x1_ref, p_ref, mod_ref, g_ref, o_ref):
    o_ref[0] = _rms(x1_ref[0] + mod_ref[0, 5:6, :] * p_ref[0], g_ref[...])


def _final(x1, peer_out, mods, g, tm):
    B, S, D = x1.shape
    tok = pl.BlockSpec((1, tm, D), lambda b, i: (b, i, 0))
    return pl.pallas_call(
        _final_kernel,
        grid=(B, S // tm),
        in_specs=[tok, tok, pl.BlockSpec((1, 6, D), lambda b, i: (b, 0, 0)),
                  pl.BlockSpec((1, D), lambda b, i: (0, 0))],
        out_specs=tok,
        out_shape=jax.ShapeDtypeStruct((B, S, D), F32),
        compiler_params=_cparams(("parallel", "parallel")),
        name="final",
    )(x1, peer_out, mods, g)


def _token_tile(S):
    return min(512, S)


def kernel(x, c, w_ada, b_ada, norm1_g, w_in, sink_a, t5_table, rpb_b, out_norm_a, out_norm_b,
           w_out, norm2_g, w_query, sub_keys, u_experts, v_experts, final_g):
    B, S, D = x.shape
    assert w_ada.shape[0] == 1, "the last residual add is fused with the final norm: one layer only"
    tm = _token_tile(S)
    mods = _ada(c, w_ada[0], b_ada[0])
    qa, ka, va, qb, kb, vb = _in_proj(x, mods, norm1_g[0].reshape(1, D), w_in[0], tm)
    o_a = _win_attn(qa, ka, va, sink_a[0], t5_table)
    o_b = _nbr_attn(qb, kb, vb, rpb_b[0])
    x1, h2 = _out_proj(o_a, o_b, x, mods, out_norm_a[0].reshape(1, W_QA), out_norm_b[0].reshape(1, W_B),
                       w_out[0], norm2_g[0].reshape(1, D), tm)
    h2f = h2.reshape(B * S, D)
    idx, gate = _route(h2f, w_query[0], sub_keys[0], min(256, B * S))
    peer_out = _peer(h2f, idx, gate, u_experts[0], v_experts[0], min(64, B * S))
    return _final(x1, peer_out.reshape(B, S, D), mods, final_g.reshape(1, D), tm)
```

```python
import functools
import math

import numpy as np
import jax
import jax.numpy as jnp
from jax import lax
from jax.experimental import pallas as pl
from jax.experimental.pallas import tpu as pltpu

F32 = jnp.float32
BF16 = jnp.bfloat16

HEAD_DIM = 64
N_HEADS_A = 8
N_KV_A = 2
N_HEADS_B = 8
WINDOW = 128
WIN_BLOCK = 128
T5_BUCKETS = 32
T5_MAX_DIST = 128
GRID_W = 64
NA_ROWS = 8
NA_COLS = 16
PEER_HEADS = 8
PEER_KEYS = 128
PEER_QDIM = 256
PEER_HALF = 128
PEER_TOPK = 16
N_PICKS = PEER_HEADS * PEER_TOPK
EPS = 1e-6
NEG = -1e30

W_QA = N_HEADS_A * HEAD_DIM
W_KVA = N_KV_A * HEAD_DIM
W_B = N_HEADS_B * HEAD_DIM

LANES = 128
D_SPLIT = 2
VMEM_LIMIT = 60 * 1024 * 1024


def _cparams(sem):
    return pltpu.CompilerParams(dimension_semantics=sem, vmem_limit_bytes=VMEM_LIMIT)


def _ada_kernel(c_ref, w_ref, b_ref, o_ref):
    c = c_ref[...]
    act = c * jax.nn.sigmoid(c)
    o_ref[...] = jnp.dot(act, w_ref[...], preferred_element_type=F32,
                         precision=lax.Precision.HIGHEST) + b_ref[...]


def _ada(c, w_ada, b_ada):
    B, D = c.shape
    n_out = w_ada.shape[1]
    rows = 8
    c_pad = jnp.zeros((rows, D), F32).at[:B].set(c)
    out = pl.pallas_call(
        _ada_kernel,
        grid=(n_out // D,),
        in_specs=[pl.BlockSpec((rows, D), lambda j: (0, 0)),
                  pl.BlockSpec((D, D), lambda j: (0, j)),
                  pl.BlockSpec((1, D), lambda j: (0, j))],
        out_specs=pl.BlockSpec((rows, D), lambda j: (0, j)),
        out_shape=jax.ShapeDtypeStruct((rows, n_out), F32),
        compiler_params=_cparams(("arbitrary",)),
        name="ada",
    )(c_pad, w_ada, b_ada.reshape(1, n_out))
    return out[:B].reshape(B, n_out // D, D)


def _rms(x, g):
    return x * lax.rsqrt(jnp.mean(x * x, axis=-1, keepdims=True) + EPS) * g


def _in_proj_kernel(x_ref, mod_ref, g_ref, wq_ref, wk_ref, wv_ref, wqb_ref, wkb_ref, wvb_ref,
                    qa_ref, ka_ref, va_ref, qb_ref, kb_ref, vb_ref):
    x = x_ref[0]
    h = _rms(x, g_ref[...]) * (1.0 + mod_ref[0, 1:2, :]) + mod_ref[0, 0:1, :]
    hb = h.astype(BF16)
    scale = HEAD_DIM ** -0.5
    qa_ref[0] = (jnp.dot(hb, wq_ref[...], preferred_element_type=F32) * scale).astype(BF16)
    ka_ref[0] = jnp.dot(hb, wk_ref[...], preferred_element_type=F32).astype(BF16)
    va_ref[0] = jnp.dot(hb, wv_ref[...], preferred_element_type=F32).astype(BF16)
    qb_ref[0] = (jnp.dot(hb, wqb_ref[...], preferred_element_type=F32) * scale).astype(BF16)
    kb_ref[0] = jnp.dot(hb, wkb_ref[...], preferred_element_type=F32).astype(BF16)
    vb_ref[0] = jnp.dot(hb, wvb_ref[...], preferred_element_type=F32).astype(BF16)


def _dup_kv_cols(w):
    a, b = w[:, :HEAD_DIM], w[:, HEAD_DIM:]
    return jnp.concatenate([a, a, b, b], axis=1)


def _in_proj(x, mods, g1, w_in, tm):
    B, S, D = x.shape
    wb = w_in.astype(BF16)
    o = 0
    wq = wb[:, o:o + W_QA]; o += W_QA
    wk = _dup_kv_cols(wb[:, o:o + W_KVA]); o += W_KVA
    wv = _dup_kv_cols(wb[:, o:o + W_KVA]); o += W_KVA
    wqb = wb[:, o:o + W_B]; o += W_B
    wkb = wb[:, o:o + W_B]; o += W_B
    wvb = wb[:, o:o + W_B]
    full = lambda a: pl.BlockSpec(a.shape, lambda b, i: (0, 0))
    tok = lambda w: pl.BlockSpec((1, tm, w), lambda b, i: (b, i, 0))
    widths = (W_QA, 2 * W_KVA, 2 * W_KVA, W_B, W_B, W_B)
    return pl.pallas_call(
        _in_proj_kernel,
        grid=(B, S // tm),
        in_specs=[tok(D), pl.BlockSpec((1, 6, D), lambda b, i: (b, 0, 0)), full(g1),
                  full(wq), full(wk), full(wv), full(wqb), full(wkb), full(wvb)],
        out_specs=[tok(w) for w in widths],
        out_shape=[jax.ShapeDtypeStruct((B, S, w), BF16) for w in widths],
        compiler_params=_cparams(("parallel", "parallel")),
        name="in_proj",
    )(x, mods, g1, wq, wk, wv, wqb, wkb, wvb)


def _t5_bucket_np(rel):
    half = T5_BUCKETS // 2
    max_exact = half // 2
    ret = np.where(rel > 0, half, 0)
    n = np.abs(rel)
    nf = np.maximum(n, 1).astype(np.float32)
    large = max_exact + (np.log(nf / np.float32(max_exact)) / np.float32(math.log(T5_MAX_DIST / max_exact))
                         * np.float32(half - max_exact)).astype(np.int32)
    large = np.minimum(large, half - 1)
    return ret + np.where(n < max_exact, n, large)


def _lookup_kernel(tab_ref, sel_ref, o_ref):
    sel = sel_ref[...]
    onehot = (lax.broadcasted_iota(jnp.int32, (tab_ref.shape[1], sel.shape[1]), 0) == sel).astype(F32)
    val = jnp.dot(tab_ref[...], onehot, preferred_element_type=F32, precision=lax.Precision.HIGHEST)
    o_ref[...] = jnp.where(sel >= 0, val, NEG)


def _lookup(table, sel):
    R, M = table.shape
    m_pad = -(-M // 8) * 8
    table = jnp.zeros((R, m_pad), F32).at[:, :M].set(table.astype(F32))
    n = sel.shape[0]
    tn = min(n, 4096)
    return pl.pallas_call(
        _lookup_kernel,
        grid=(n // tn,),
        in_specs=[pl.BlockSpec((R, m_pad), lambda j: (0, 0)), pl.BlockSpec((1, tn), lambda j: (0, j))],
        out_specs=pl.BlockSpec((R, tn), lambda j: (0, j)),
        out_shape=jax.ShapeDtypeStruct((R, n), F32),
        compiler_params=_cparams(("parallel",)),
        name="bias_lookup",
    )(table, jnp.asarray(sel.reshape(1, n), jnp.int32))


def _softmax_rows(s, extra=None):
    m = jnp.max(s, axis=-1, keepdims=True)
    if extra is not None:
        m = jnp.maximum(m, extra)
    e = jnp.exp(s - m)
    den = jnp.sum(e, axis=-1, keepdims=True)
    if extra is not None:
        den = den + jnp.exp(extra - m)
    return e / den


def _win_attn_kernel(sink_ref, q_ref, kp_ref, kc_ref, kn_ref, vp_ref, vc_ref, vn_ref, bias_ref, o_ref, *, seq):
    i = pl.program_id(1)
    q = q_ref[0]
    k = jnp.concatenate([kp_ref[0], kc_ref[0], kn_ref[0]], axis=0)
    v = jnp.concatenate([vp_ref[0], vc_ref[0], vn_ref[0]], axis=0)
    span = k.shape[0]
    kpos = lax.broadcasted_iota(jnp.int32, (WIN_BLOCK, span), 1) + (i - 1) * WIN_BLOCK
    valid = (kpos >= 0) & (kpos < seq)
    low = lax.broadcasted_iota(jnp.int32, (WIN_BLOCK, LANES), 1) < HEAD_DIM
    zero = jnp.zeros((WIN_BLOCK, LANES), BF16)
    for pair in range(N_HEADS_A // 2):
        qp = q[:, pair * LANES:(pair + 1) * LANES]
        outs = []
        for par in range(2):
            h = 2 * pair + par
            grp = h // (N_HEADS_A // N_KV_A)
            kg = k[:, grp * LANES:(grp + 1) * LANES]
            vg = v[:, grp * LANES:(grp + 1) * LANES]
            qh = jnp.where(low if par == 0 else jnp.logical_not(low), qp, zero)
            s = lax.dot_general(qh, kg, (((1,), (1,)), ((), ())), preferred_element_type=F32)
            s = jnp.where(valid, s + bias_ref[h], NEG)
            p = _softmax_rows(s, extra=sink_ref[h]).astype(BF16)
            outs.append(jnp.dot(p, vg, preferred_element_type=F32))
        o_ref[0, :, pair * LANES:(pair + 1) * LANES] = jnp.where(low, outs[0], outs[1])


def _win_attn(qa, ka, va, sink, t5_table):
    B, S, _ = qa.shape
    nb = S // WIN_BLOCK
    span = WIN_BLOCK + 2 * WINDOW
    rel = np.arange(span)[None, :] - WINDOW - np.arange(WIN_BLOCK)[:, None]
    bucket = np.where(np.abs(rel) <= WINDOW, _t5_bucket_np(rel), -1)
    bias = _lookup(t5_table.T, bucket.reshape(-1)).reshape(N_HEADS_A, WIN_BLOCK, span)
    qspec = pl.BlockSpec((1, WIN_BLOCK, W_QA), lambda b, i: (b, i, 0))
    kv = lambda off: pl.BlockSpec((1, WIN_BLOCK, 2 * W_KVA),
                                  lambda b, i: (b, jnp.clip(i + off, 0, nb - 1), 0))
    return pl.pallas_call(
        functools.partial(_win_attn_kernel, seq=S),
        grid=(B, nb),
        in_specs=[pl.BlockSpec(memory_space=pltpu.SMEM), qspec,
                  kv(-1), kv(0), kv(1), kv(-1), kv(0), kv(1),
                  pl.BlockSpec(bias.shape, lambda b, i: (0, 0, 0))],
        out_specs=pl.BlockSpec((1, WIN_BLOCK, W_QA), lambda b, i: (b, i, 0)),
        out_shape=jax.ShapeDtypeStruct((B, S, W_QA), F32),
        compiler_params=_cparams(("parallel", "parallel")),
        name="win_attn",
    )(sink.astype(F32), qa, ka, ka, ka, va, va, va, bias)


def _nbr_attn_kernel(*refs):
    q_ref = refs[0]
    k_refs = refs[1:1 + NA_ROWS]
    v_refs = refs[1 + NA_ROWS:1 + 2 * NA_ROWS]
    bias_ref = refs[1 + 2 * NA_ROWS]
    o_ref = refs[2 + 2 * NA_ROWS]
    q = q_ref[0]
    k = jnp.concatenate([r[0] for r in k_refs], axis=0)
    v = jnp.concatenate([r[0] for r in v_refs], axis=0)
    low = lax.broadcasted_iota(jnp.int32, (GRID_W, LANES), 1) < HEAD_DIM
    zero = jnp.zeros((GRID_W, LANES), BF16)
    for pair in range(N_HEADS_B // 2):
        sl = slice(pair * LANES, (pair + 1) * LANES)
        qp, kp, vp = q[:, sl], k[:, sl], v[:, sl]
        outs = []
        for par in range(2):
            h = 2 * pair + par
            qh = jnp.where(low if par == 0 else jnp.logical_not(low), qp, zero)
            s = lax.dot_general(qh, kp, (((1,), (1,)), ((), ())), preferred_element_type=F32)
            p = _softmax_rows(s + bias_ref[0, h]).astype(BF16)
            outs.append(jnp.dot(p, vp, preferred_element_type=F32))
        o_ref[0, :, sl] = jnp.where(low, outs[0], outs[1])


def _nbr_bias(rpb):
    cols = np.arange(GRID_W)
    col_start = np.clip(cols - NA_COLS // 2, 0, GRID_W - NA_COLS)
    kc = np.arange(GRID_W)
    inside = (kc[None, :] >= col_start[:, None]) & (kc[None, :] < col_start[:, None] + NA_COLS)
    col_off = np.where(inside, kc[None, :] - cols[:, None] + (NA_COLS - 1), -1)
    table = jnp.stack([rpb[:, NA_ROWS - 1 - d:2 * NA_ROWS - 1 - d, :] for d in range(NA_ROWS)])
    b = _lookup(table.reshape(NA_ROWS * N_HEADS_B * NA_ROWS, 2 * NA_COLS - 1), col_off.reshape(-1))
    b = b.reshape(NA_ROWS, N_HEADS_B, NA_ROWS, GRID_W, GRID_W).transpose(0, 1, 3, 2, 4)
    return b.reshape(NA_ROWS, N_HEADS_B, GRID_W, NA_ROWS * GRID_W)


def _nbr_attn(qb, kb, vb, rpb):
    B, S, _ = qb.shape
    rows = S // GRID_W
    assert rows >= NA_ROWS
    bias = _nbr_bias(rpb)
    band_start = lambda r: jnp.clip(r - NA_ROWS // 2, 0, rows - NA_ROWS)
    qspec = pl.BlockSpec((1, GRID_W, W_B), lambda b, r: (b, r, 0))
    kvspec = lambda j: pl.BlockSpec((1, GRID_W, W_B), lambda b, r: (b, band_start(r) + j, 0))
    return pl.pallas_call(
        _nbr_attn_kernel,
        grid=(B, rows),
        in_specs=[qspec] + [kvspec(j) for j in range(NA_ROWS)] * 2
                 + [pl.BlockSpec((1,) + bias.shape[1:], lambda b, r: (r - band_start(r), 0, 0, 0))],
        out_specs=pl.BlockSpec((1, GRID_W, W_B), lambda b, r: (b, r, 0)),
        out_shape=jax.ShapeDtypeStruct((B, S, W_B), F32),
        compiler_params=_cparams(("parallel", "parallel")),
        name="nbr_attn",
    )(qb, *([kb] * NA_ROWS), *([vb] * NA_ROWS), bias)


def _out_proj_kernel(oa_ref, ob_ref, x_ref, mod_ref, ga_ref, gb_ref, wa_ref, wb_ref, g2_ref, x1_ref, h2_ref):
    na = _rms(oa_ref[0], ga_ref[...]).astype(BF16)
    nb = _rms(ob_ref[0], gb_ref[...]).astype(BF16)
    y = (jnp.dot(na, wa_ref[...], preferred_element_type=F32)
         + jnp.dot(nb, wb_ref[...], preferred_element_type=F32))
    x1 = x_ref[0] + mod_ref[0, 2:3, :] * y
    x1_ref[0] = x1
    h2_ref[0] = _rms(x1, g2_ref[...]) * (1.0 + mod_ref[0, 4:5, :]) + mod_ref[0, 3:4, :]


def _out_proj(o_a, o_b, x, mods, ga, gb, w_out, g2, tm):
    B, S, D = x.shape
    wb16 = w_out.astype(BF16)
    wa, wb = wb16[:W_QA], wb16[W_QA:]
    full = lambda a: pl.BlockSpec(a.shape, lambda b, i: (0, 0))
    tok = lambda w: pl.BlockSpec((1, tm, w), lambda b, i: (b, i, 0))
    return pl.pallas_call(
        _out_proj_kernel,
        grid=(B, S // tm),
        in_specs=[tok(W_QA), tok(W_B), tok(D), pl.BlockSpec((1, 6, D), lambda b, i: (b, 0, 0)),
                  full(ga), full(gb), full(wa), full(wb), full(g2)],
        out_specs=[tok(D), tok(D)],
        out_shape=[jax.ShapeDtypeStruct((B, S, D), F32)] * 2,
        compiler_params=_cparams(("parallel", "parallel")),
        name="out_proj",
    )(o_a, o_b, x, mods, ga, gb, wa, wb, g2)


def _topk_rows(s, k, payload=None):
    n_rows = s.shape[0]
    row = lax.broadcasted_iota(jnp.int32, s.shape, 0)
    vals, picked = [], []
    for _ in range(k):
        m = jnp.max(s, axis=0, keepdims=True)
        am = jnp.min(jnp.where(s == m, row, n_rows), axis=0, keepdims=True)
        hit = row == am
        vals.append(m)
        picked.append(am if payload is None else jnp.sum(jnp.where(hit, payload, 0), axis=0, keepdims=True))
        s = jnp.where(hit, -jnp.inf, s)
    return jnp.concatenate(vals, axis=0), jnp.concatenate(picked, axis=0)


def _product_candidates(v1, i1, v2, i2):
    assert PEER_TOPK == 16
    n = v1.shape[1]
    vals, ids = [], []
    for a in range(8):
        nb = PEER_TOPK if a == 0 else 8
        sv = v1[a:a + 1, :] + v2[:nb, :]
        limit = PEER_TOPK // (a + 1)
        if limit < nb:
            sv = jnp.where(lax.broadcasted_iota(jnp.int32, (nb, n), 0) < limit, sv, -jnp.inf)
        vals.append(sv)
        ids.append(i1[a:a + 1, :] * PEER_KEYS + i2[:nb, :])
    vals.append(v1[8:, :] + v2[0:1, :])
    ids.append(i1[8:, :] * PEER_KEYS + i2[0:1, :])
    return jnp.concatenate(vals, axis=0), jnp.concatenate(ids, axis=0)


def _route_kernel(h_ref, wq_ref, keys_ref, idx_ref, gate_ref):
    hb = h_ref[...].astype(BF16)
    idx_rows, gate_rows = [], []
    for h in range(PEER_HEADS):
        tops = []
        for side in range(2):
            c0 = h * PEER_QDIM + side * PEER_HALF
            qh = jnp.dot(hb, wq_ref[:, c0:c0 + PEER_HALF], preferred_element_type=F32).astype(BF16)
            s = lax.dot_general(keys_ref[h, side], qh, (((1,), (1,)), ((), ())),
                                preferred_element_type=F32)
            tops.append(_topk_rows(s, PEER_TOPK))
        (v1, i1), (v2, i2) = tops
        cand, cand_ids = _product_candidates(v1, i1, v2, i2)
        vs, ids = _topk_rows(cand, PEER_TOPK, payload=cand_ids)
        idx_rows.append(ids)
        ex = jnp.exp(vs - jnp.max(vs, axis=0, keepdims=True))
        gate_rows.append(ex / jnp.sum(ex, axis=0, keepdims=True))
    idx_ref[...] = (jnp.concatenate(idx_rows, axis=0) * SLAB_ROWS).T
    gate_ref[...] = jnp.concatenate(gate_rows, axis=0).T


def _route(h2, w_query, sub_keys, tm):
    T, D = h2.shape
    wq = w_query.astype(BF16)
    keys = sub_keys.astype(BF16)
    return pl.pallas_call(
        _route_kernel,
        grid=(T // tm,),
        in_specs=[pl.BlockSpec((tm, D), lambda i: (i, 0)),
                  pl.BlockSpec(wq.shape, lambda i: (0, 0)),
                  pl.BlockSpec(keys.shape, lambda i: (0, 0, 0, 0))],
        out_specs=[pl.BlockSpec((tm, N_PICKS), lambda i: (i, 0))] * 2,
        out_shape=[jax.ShapeDtypeStruct((T, N_PICKS), jnp.int32),
                   jax.ShapeDtypeStruct((T, N_PICKS), F32)],
        compiler_params=_cparams(("parallel",)),
        name="route",
    )(h2, wq, keys)


def _eye(n):
    return lax.broadcasted_iota(jnp.int32, (n, n), 0) == lax.broadcasted_iota(jnp.int32, (n, n), 1)


def _slab_sums8(slabs):
    p = slabs
    merged = None
    row4 = lax.broadcasted_iota(jnp.int32, (8, LANES), 0) % 4
    for r, (first, second) in enumerate(((p[3], p[7]), (p[4], p[0]), (p[5], p[1]), (p[6], p[2]))):
        f = jnp.concatenate([first, second], axis=0)
        f = f + pltpu.roll(f, 2, axis=0)
        f = f + pltpu.roll(f, 1, axis=0)
        if r:
            f = pltpu.roll(f, r, axis=0)
        merged = f if merged is None else jnp.where(row4 == (3 + r) % 4, f, merged)
    return merged


SLAB_ROWS = 4
U_TOKENS_PER_ITER = 4
V_TOKENS_PER_BANK = 2


def _expert_slab(tbl_ref, row):
    return tbl_ref[pl.ds(pl.multiple_of(row, SLAB_ROWS), SLAB_ROWS), :]


def _peer_u_kernel(rows_ref, x_ref, tbl_ref, z_ref):
    eye = _eye(N_PICKS)

    def token(t):
        x = x_ref[t]
        picks = rows_ref.at[t]
        prods = [_expert_slab(tbl_ref, picks[k]) * x for k in range(N_PICKS)]
        sums = jnp.concatenate([_slab_sums8(prods[8 * g:8 * g + 8]) for g in range(N_PICKS // 8)], axis=0)
        col = jnp.sum(sums, axis=1, keepdims=True)
        z_ref[pl.ds(t, 1), :] = jnp.sum(jnp.where(eye, col, 0.0), axis=0, keepdims=True)

    def body(i, carry):
        for u in range(U_TOKENS_PER_ITER):
            token(i * U_TOKENS_PER_ITER + u)
        return carry

    lax.fori_loop(0, x_ref.shape[0] // U_TOKENS_PER_ITER, body, 0)


def _gelu_tanh(z):
    return 0.5 * z * (1.0 + jnp.tanh(math.sqrt(2.0 / math.pi) * (z + 0.044715 * (z * z * z))))


def _peer_v_kernel(rows_ref, z_ref, gate_ref, tbl_ref, o_ref, a_scr, w_scr):
    tb = o_ref.shape[0]
    z = z_ref[0]
    for part in range(1, z_ref.shape[0]):
        z = z + z_ref[part]
    a_scr[...] = _gelu_tanh(z) * gate_ref[...]
    eye = _eye(N_PICKS)
    n_acc = 4
    nb = V_TOKENS_PER_BANK

    def spread(t, slot):
        a_row = a_scr[pl.ds(jnp.minimum(t, tb - 1), 1), :]
        col = jnp.sum(jnp.where(eye, a_row, 0.0), axis=1, keepdims=True)
        w_scr[slot] = jnp.broadcast_to(col, (N_PICKS, LANES))

    def gather(t, slot):
        picks = rows_ref.at[t]
        accs = [None] * n_acc
        for k in range(N_PICKS):
            term = w_scr[slot, pl.ds(k, 1), :] * _expert_slab(tbl_ref, picks[k])
            accs[k % n_acc] = term if accs[k % n_acc] is None else accs[k % n_acc] + term
        o_ref[t] = (accs[0] + accs[1]) + (accs[2] + accs[3])

    for u in range(nb):
        spread(u, u)

    def body(i, carry):
        base = i * 2 * nb
        for u in range(nb):
            spread(base + nb + u, nb + u)
        for u in range(nb):
            gather(base + u, u)
        for u in range(nb):
            spread(base + 2 * nb + u, u)
        for u in range(nb):
            gather(base + nb + u, nb + u)
        return carry

    lax.fori_loop(0, tb // (2 * nb), body, 0)


def _expert_table_spec(n_rows):
    return pl.BlockSpec((None, n_rows, LANES), lambda h, i: (h, 0, 0), pipeline_mode=pl.Buffered(1))


def _half_tables(experts):
    n_experts, D = experts.shape
    t = experts.reshape(n_experts, D_SPLIT, SLAB_ROWS, LANES).transpose(1, 0, 2, 3)
    return t.reshape(D_SPLIT, n_experts * SLAB_ROWS, LANES)


def _peer(h2, rows, gate, u_experts, v_experts, tb):
    T, D = h2.shape
    assert D == D_SPLIT * SLAB_ROWS * LANES
    assert tb % U_TOKENS_PER_ITER == 0 and tb % (2 * V_TOKENS_PER_BANK) == 0
    n_rows = u_experts.shape[0] * SLAB_ROWS
    rows_spec = pl.BlockSpec((tb, N_PICKS), lambda h, i: (i, 0), memory_space=pltpu.SMEM)
    tok_slab = pl.BlockSpec((tb, None, SLAB_ROWS, LANES), lambda h, i: (i, h, 0, 0))
    z = pl.pallas_call(
        _peer_u_kernel,
        grid=(D_SPLIT, T // tb),
        in_specs=[rows_spec, tok_slab, _expert_table_spec(n_rows)],
        out_specs=pl.BlockSpec((None, tb, N_PICKS), lambda h, i: (h, i, 0)),
        out_shape=jax.ShapeDtypeStruct((D_SPLIT, T, N_PICKS), F32),
        compiler_params=_cparams(("arbitrary", "arbitrary")),
        name="peer_u",
    )(rows, h2.reshape(T, D_SPLIT, SLAB_ROWS, LANES), _half_tables(u_experts))
    out = pl.pallas_call(
        _peer_v_kernel,
        grid=(D_SPLIT, T // tb),
        in_specs=[rows_spec,
                  pl.BlockSpec((D_SPLIT, tb, N_PICKS), lambda h, i: (0, i, 0)),
                  pl.BlockSpec((tb, N_PICKS), lambda h, i: (i, 0)),
                  _expert_table_spec(n_rows)],
        out_specs=tok_slab,
        out_shape=jax.ShapeDtypeStruct((T, D_SPLIT, SLAB_ROWS, LANES), F32),
        scratch_shapes=[pltpu.VMEM((tb, N_PICKS), F32),
                        pltpu.VMEM((2 * V_TOKENS_PER_BANK, N_PICKS, LANES), F32)],
        compiler_params=_cparams(("arbitrary", "arbitrary")),
        name="peer_v",
    )(rows, z, gate, _half_tables(v_experts))
    return out.reshape(T, D)


def _final_kernel(x1_ref, p_ref, mod_ref, g_ref, o_ref):
    o_ref[0] = _rms(x1_ref[0] + mod_ref[0, 5:6, :] * p_ref[0], g_ref[...])


def _final(x1, peer_out, mods, g, tm):
    B, S, D = x1.shape
    tok = pl.BlockSpec((1, tm, D), lambda b, i: (b, i, 0))
    return pl.pallas_call(
        _final_kernel,
        grid=(B, S // tm),
        in_specs=[tok, tok, pl.BlockSpec((1, 6, D), lambda b, i: (b, 0, 0)),
                  pl.BlockSpec((1, D), lambda b, i: (0, 0))],
        out_specs=tok,
        out_shape=jax.ShapeDtypeStruct((B, S, D), F32),
        compiler_params=_cparams(("parallel", "parallel")),
        name="final",
    )(x1, peer_out, mods, g)


def _token_tile(S):
    return min(512, S)


def kernel(x, c, w_ada, b_ada, norm1_g, w_in, sink_a, t5_table, rpb_b, out_norm_a, out_norm_b,
           w_out, norm2_g, w_query, sub_keys, u_experts, v_experts, final_g):
    B, S, D = x.shape
    assert w_ada.shape[0] == 1, "the last residual add is fused with the final norm: one layer only"
    tm = _token_tile(S)
    mods = _ada(c, w_ada[0], b_ada[0])
    qa, ka, va, qb, kb, vb = _in_proj(x, mods, norm1_g[0].reshape(1, D), w_in[0], tm)
    o_a = _win_attn(qa, ka, va, sink_a[0], t5_table)
    o_b = _nbr_attn(qb, kb, vb, rpb_b[0])
    x1, h2 = _out_proj(o_a, o_b, x, mods, out_norm_a[0].reshape(1, W_QA), out_norm_b[0].reshape(1, W_B),
                       w_out[0], norm2_g[0].reshape(1, D), tm)
    h2f = h2.reshape(B * S, D)
    rows, gate = _route(h2f, w_query[0], sub_keys[0], min(256, B * S))
    peer_out = _peer(h2f, rows, gate, u_experts[0], v_experts[0], min(64, B * S))
    return _final(x1, peer_out.reshape(B, S, D), mods, final_g.reshape(1, D), tm)
```

```python
import functools
import math

import numpy as np
import jax
import jax.numpy as jnp
from jax import lax
from jax.experimental import pallas as pl
from jax.experimental.pallas import tpu as pltpu

F32 = jnp.float32
BF16 = jnp.bfloat16

HEAD_DIM = 64
N_HEADS_A = 8
N_KV_A = 2
N_HEADS_B = 8
WINDOW = 128
WIN_BLOCK = 128
T5_BUCKETS = 32
T5_MAX_DIST = 128
GRID_W = 64
NA_ROWS = 8
NA_COLS = 16
PEER_HEADS = 8
PEER_KEYS = 128
PEER_QDIM = 256
PEER_HALF = 128
PEER_TOPK = 16
N_PICKS = PEER_HEADS * PEER_TOPK
EPS = 1e-6
NEG = -1e30

W_QA = N_HEADS_A * HEAD_DIM
W_KVA = N_KV_A * HEAD_DIM
W_B = N_HEADS_B * HEAD_DIM

LANES = 128
VMEM_LIMIT = 60 * 1024 * 1024


def _cparams(sem):
    return pltpu.CompilerParams(dimension_semantics=sem, vmem_limit_bytes=VMEM_LIMIT)


def _ada_kernel(c_ref, w_ref, b_ref, o_ref):
    c = c_ref[...]
    act = c * jax.nn.sigmoid(c)
    o_ref[...] = jnp.dot(act, w_ref[...], preferred_element_type=F32,
                         precision=lax.Precision.HIGHEST) + b_ref[...]


def _ada(c, w_ada, b_ada):
    B, D = c.shape
    n_out = w_ada.shape[1]
    rows = 8
    c_pad = jnp.zeros((rows, D), F32).at[:B].set(c)
    out = pl.pallas_call(
        _ada_kernel,
        grid=(n_out // D,),
        in_specs=[pl.BlockSpec((rows, D), lambda j: (0, 0)),
                  pl.BlockSpec((D, D), lambda j: (0, j)),
                  pl.BlockSpec((1, D), lambda j: (0, j))],
        out_specs=pl.BlockSpec((rows, D), lambda j: (0, j)),
        out_shape=jax.ShapeDtypeStruct((rows, n_out), F32),
        compiler_params=_cparams(("arbitrary",)),
        name="ada",
    )(c_pad, w_ada, b_ada.reshape(1, n_out))
    return out[:B].reshape(B, n_out // D, D)


def _rms(x, g):
    return x * lax.rsqrt(jnp.mean(x * x, axis=-1, keepdims=True) + EPS) * g


def _in_proj_kernel(x_ref, mod_ref, g_ref, wq_ref, wk_ref, wv_ref, wqb_ref, wkb_ref, wvb_ref,
                    qa_ref, ka_ref, va_ref, qb_ref, kb_ref, vb_ref):
    x = x_ref[0]
    h = _rms(x, g_ref[...]) * (1.0 + mod_ref[0, 1:2, :]) + mod_ref[0, 0:1, :]
    hb = h.astype(BF16)
    scale = HEAD_DIM ** -0.5
    qa_ref[0] = (jnp.dot(hb, wq_ref[...], preferred_element_type=F32) * scale).astype(BF16)
    ka_ref[0] = jnp.dot(hb, wk_ref[...], preferred_element_type=F32).astype(BF16)
    va_ref[0] = jnp.dot(hb, wv_ref[...], preferred_element_type=F32).astype(BF16)
    qb_ref[0] = (jnp.dot(hb, wqb_ref[...], preferred_element_type=F32) * scale).astype(BF16)
    kb_ref[0] = jnp.dot(hb, wkb_ref[...], preferred_element_type=F32).astype(BF16)
    vb_ref[0] = jnp.dot(hb, wvb_ref[...], preferred_element_type=F32).astype(BF16)


def _dup_kv_cols(w):
    a, b = w[:, :HEAD_DIM], w[:, HEAD_DIM:]
    return jnp.concatenate([a, a, b, b], axis=1)


def _in_proj(x, mods, g1, w_in, tm):
    B, S, D = x.shape
    wb = w_in.astype(BF16)
    o = 0
    wq = wb[:, o:o + W_QA]; o += W_QA
    wk = _dup_kv_cols(wb[:, o:o + W_KVA]); o += W_KVA
    wv = _dup_kv_cols(wb[:, o:o + W_KVA]); o += W_KVA
    wqb = wb[:, o:o + W_B]; o += W_B
    wkb = wb[:, o:o + W_B]; o += W_B
    wvb = wb[:, o:o + W_B]
    full = lambda a: pl.BlockSpec(a.shape, lambda b, i: (0, 0))
    tok = lambda w: pl.BlockSpec((1, tm, w), lambda b, i: (b, i, 0))
    widths = (W_QA, 2 * W_KVA, 2 * W_KVA, W_B, W_B, W_B)
    return pl.pallas_call(
        _in_proj_kernel,
        grid=(B, S // tm),
        in_specs=[tok(D), pl.BlockSpec((1, 6, D), lambda b, i: (b, 0, 0)), full(g1),
                  full(wq), full(wk), full(wv), full(wqb), full(wkb), full(wvb)],
        out_specs=[tok(w) for w in widths],
        out_shape=[jax.ShapeDtypeStruct((B, S, w), BF16) for w in widths],
        compiler_params=_cparams(("parallel", "parallel")),
        name="in_proj",
    )(x, mods, g1, wq, wk, wv, wqb, wkb, wvb)


def _t5_bucket_np(rel):
    half = T5_BUCKETS // 2
    max_exact = half // 2
    ret = np.where(rel > 0, half, 0)
    n = np.abs(rel)
    nf = np.maximum(n, 1).astype(np.float32)
    large = max_exact + (np.log(nf / np.float32(max_exact)) / np.float32(math.log(T5_MAX_DIST / max_exact))
                         * np.float32(half - max_exact)).astype(np.int32)
    large = np.minimum(large, half - 1)
    return ret + np.where(n < max_exact, n, large)


def _lookup_kernel(tab_ref, sel_ref, o_ref):
    sel = sel_ref[...]
    onehot = (lax.broadcasted_iota(jnp.int32, (tab_ref.shape[1], sel.shape[1]), 0) == sel).astype(F32)
    val = jnp.dot(tab_ref[...], onehot, preferred_element_type=F32, precision=lax.Precision.HIGHEST)
    o_ref[...] = jnp.where(sel >= 0, val, NEG)


def _lookup(table, sel):
    R, M = table.shape
    m_pad = -(-M // 8) * 8
    table = jnp.zeros((R, m_pad), F32).at[:, :M].set(table.astype(F32))
    n = sel.shape[0]
    tn = min(n, 4096)
    return pl.pallas_call(
        _lookup_kernel,
        grid=(n // tn,),
        in_specs=[pl.BlockSpec((R, m_pad), lambda j: (0, 0)), pl.BlockSpec((1, tn), lambda j: (0, j))],
        out_specs=pl.BlockSpec((R, tn), lambda j: (0, j)),
        out_shape=jax.ShapeDtypeStruct((R, n), F32),
        compiler_params=_cparams(("parallel",)),
        name="bias_lookup",
    )(table, jnp.asarray(sel.reshape(1, n), jnp.int32))


def _softmax_rows(s, extra=None):
    m = jnp.max(s, axis=-1, keepdims=True)
    if extra is not None:
        m = jnp.maximum(m, extra)
    e = jnp.exp(s - m)
    den = jnp.sum(e, axis=-1, keepdims=True)
    if extra is not None:
        den = den + jnp.exp(extra - m)
    return e / den


def _win_attn_kernel(sink_ref, q_ref, kp_ref, kc_ref, kn_ref, vp_ref, vc_ref, vn_ref, bias_ref, o_ref, *, seq):
    i = pl.program_id(1)
    q = q_ref[0]
    k = jnp.concatenate([kp_ref[0], kc_ref[0], kn_ref[0]], axis=0)
    v = jnp.concatenate([vp_ref[0], vc_ref[0], vn_ref[0]], axis=0)
    span = k.shape[0]
    kpos = lax.broadcasted_iota(jnp.int32, (WIN_BLOCK, span), 1) + (i - 1) * WIN_BLOCK
    valid = (kpos >= 0) & (kpos < seq)
    low = lax.broadcasted_iota(jnp.int32, (WIN_BLOCK, LANES), 1) < HEAD_DIM
    zero = jnp.zeros((WIN_BLOCK, LANES), BF16)
    for pair in range(N_HEADS_A // 2):
        qp = q[:, pair * LANES:(pair + 1) * LANES]
        outs = []
        for par in range(2):
            h = 2 * pair + par
            grp = h // (N_HEADS_A // N_KV_A)
            kg = k[:, grp * LANES:(grp + 1) * LANES]
            vg = v[:, grp * LANES:(grp + 1) * LANES]
            qh = jnp.where(low if par == 0 else jnp.logical_not(low), qp, zero)
            s = lax.dot_general(qh, kg, (((1,), (1,)), ((), ())), preferred_element_type=F32)
            s = jnp.where(valid, s + bias_ref[h], NEG)
            p = _softmax_rows(s, extra=sink_ref[h]).astype(BF16)
            outs.append(jnp.dot(p, vg, preferred_element_type=F32))
        o_ref[0, :, pair * LANES:(pair + 1) * LANES] = jnp.where(low, outs[0], outs[1])


def _win_attn(qa, ka, va, sink, t5_table):
    B, S, _ = qa.shape
    nb = S // WIN_BLOCK
    span = WIN_BLOCK + 2 * WINDOW
    rel = np.arange(span)[None, :] - WINDOW - np.arange(WIN_BLOCK)[:, None]
    bucket = np.where(np.abs(rel) <= WINDOW, _t5_bucket_np(rel), -1)
    bias = _lookup(t5_table.T, bucket.reshape(-1)).reshape(N_HEADS_A, WIN_BLOCK, span)
    qspec = pl.BlockSpec((1, WIN_BLOCK, W_QA), lambda b, i: (b, i, 0))
    kv = lambda off: pl.BlockSpec((1, WIN_BLOCK, 2 * W_KVA),
                                  lambda b, i: (b, jnp.clip(i + off, 0, nb - 1), 0))
    return pl.pallas_call(
        functools.partial(_win_attn_kernel, seq=S),
        grid=(B, nb),
        in_specs=[pl.BlockSpec(memory_space=pltpu.SMEM), qspec,
                  kv(-1), kv(0), kv(1), kv(-1), kv(0), kv(1),
                  pl.BlockSpec(bias.shape, lambda b, i: (0, 0, 0))],
        out_specs=pl.BlockSpec((1, WIN_BLOCK, W_QA), lambda b, i: (b, i, 0)),
        out_shape=jax.ShapeDtypeStruct((B, S, W_QA), F32),
        compiler_params=_cparams(("parallel", "parallel")),
        name="win_attn",
    )(sink.astype(F32), qa, ka, ka, ka, va, va, va, bias)


def _nbr_attn_kernel(*refs):
    q_ref = refs[0]
    k_refs = refs[1:1 + NA_ROWS]
    v_refs = refs[1 + NA_ROWS:1 + 2 * NA_ROWS]
    bias_ref = refs[1 + 2 * NA_ROWS]
    o_ref = refs[2 + 2 * NA_ROWS]
    q = q_ref[0]
    k = jnp.concatenate([r[0] for r in k_refs], axis=0)
    v = jnp.concatenate([r[0] for r in v_refs], axis=0)
    low = lax.broadcasted_iota(jnp.int32, (GRID_W, LANES), 1) < HEAD_DIM
    zero = jnp.zeros((GRID_W, LANES), BF16)
    for pair in range(N_HEADS_B // 2):
        sl = slice(pair * LANES, (pair + 1) * LANES)
        qp, kp, vp = q[:, sl], k[:, sl], v[:, sl]
        outs = []
        for par in range(2):
            h = 2 * pair + par
            qh = jnp.where(low if par == 0 else jnp.logical_not(low), qp, zero)
            s = lax.dot_general(qh, kp, (((1,), (1,)), ((), ())), preferred_element_type=F32)
            p = _softmax_rows(s + bias_ref[0, h]).astype(BF16)
            outs.append(jnp.dot(p, vp, preferred_element_type=F32))
        o_ref[0, :, sl] = jnp.where(low, outs[0], outs[1])


def _nbr_bias(rpb):
    cols = np.arange(GRID_W)
    col_start = np.clip(cols - NA_COLS // 2, 0, GRID_W - NA_COLS)
    kc = np.arange(GRID_W)
    inside = (kc[None, :] >= col_start[:, None]) & (kc[None, :] < col_start[:, None] + NA_COLS)
    col_off = np.where(inside, kc[None, :] - cols[:, None] + (NA_COLS - 1), -1)
    table = jnp.stack([rpb[:, NA_ROWS - 1 - d:2 * NA_ROWS - 1 - d, :] for d in range(NA_ROWS)])
    b = _lookup(table.reshape(NA_ROWS * N_HEADS_B * NA_ROWS, 2 * NA_COLS - 1), col_off.reshape(-1))
    b = b.reshape(NA_ROWS, N_HEADS_B, NA_ROWS, GRID_W, GRID_W).transpose(0, 1, 3, 2, 4)
    return b.reshape(NA_ROWS, N_HEADS_B, GRID_W, NA_ROWS * GRID_W)


def _nbr_attn(qb, kb, vb, rpb):
    B, S, _ = qb.shape
    rows = S // GRID_W
    assert rows >= NA_ROWS
    bias = _nbr_bias(rpb)
    band_start = lambda r: jnp.clip(r - NA_ROWS // 2, 0, rows - NA_ROWS)
    qspec = pl.BlockSpec((1, GRID_W, W_B), lambda b, r: (b, r, 0))
    kvspec = lambda j: pl.BlockSpec((1, GRID_W, W_B), lambda b, r: (b, band_start(r) + j, 0))
    return pl.pallas_call(
        _nbr_attn_kernel,
        grid=(B, rows),
        in_specs=[qspec] + [kvspec(j) for j in range(NA_ROWS)] * 2
                 + [pl.BlockSpec((1,) + bias.shape[1:], lambda b, r: (r - band_start(r), 0, 0, 0))],
        out_specs=pl.BlockSpec((1, GRID_W, W_B), lambda b, r: (b, r, 0)),
        out_shape=jax.ShapeDtypeStruct((B, S, W_B), F32),
        compiler_params=_cparams(("parallel", "parallel")),
        name="nbr_attn",
    )(qb, *([kb] * NA_ROWS), *([vb] * NA_ROWS), bias)


def _out_proj_kernel(oa_ref, ob_ref, x_ref, mod_ref, ga_ref, gb_ref, wa_ref, wb_ref, g2_ref, x1_ref, h2_ref):
    na = _rms(oa_ref[0], ga_ref[...]).astype(BF16)
    nb = _rms(ob_ref[0], gb_ref[...]).astype(BF16)
    y = (jnp.dot(na, wa_ref[...], preferred_element_type=F32)
         + jnp.dot(nb, wb_ref[...], preferred_element_type=F32))
    x1 = x_ref[0] + mod_ref[0, 2:3, :] * y
    x1_ref[0] = x1
    h2_ref[0] = _rms(x1, g2_ref[...]) * (1.0 + mod_ref[0, 4:5, :]) + mod_ref[0, 3:4, :]


def _out_proj(o_a, o_b, x, mods, ga, gb, w_out, g2, tm):
    B, S, D = x.shape
    wb16 = w_out.astype(BF16)
    wa, wb = wb16[:W_QA], wb16[W_QA:]
    full = lambda a: pl.BlockSpec(a.shape, lambda b, i: (0, 0))
    tok = lambda w: pl.BlockSpec((1, tm, w), lambda b, i: (b, i, 0))
    return pl.pallas_call(
        _out_proj_kernel,
        grid=(B, S // tm),
        in_specs=[tok(W_QA), tok(W_B), tok(D), pl.BlockSpec((1, 6, D), lambda b, i: (b, 0, 0)),
                  full(ga), full(gb), full(wa), full(wb), full(g2)],
        out_specs=[tok(D), tok(D)],
        out_shape=[jax.ShapeDtypeStruct((B, S, D), F32)] * 2,
        compiler_params=_cparams(("parallel", "parallel")),
        name="out_proj",
    )(o_a, o_b, x, mods, ga, gb, wa, wb, g2)


def _topk_rows(s, k, payload=None):
    n_rows = s.shape[0]
    row = lax.broadcasted_iota(jnp.int32, s.shape, 0)
    vals, picked = [], []
    for _ in range(k):
        m = jnp.max(s, axis=0, keepdims=True)
        am = jnp.min(jnp.where(s == m, row, n_rows), axis=0, keepdims=True)
        hit = row == am
        vals.append(m)
        picked.append(am if payload is None else jnp.sum(jnp.where(hit, payload, 0), axis=0, keepdims=True))
        s = jnp.where(hit, -jnp.inf, s)
    return jnp.concatenate(vals, axis=0), jnp.concatenate(picked, axis=0)


def _product_candidates(v1, i1, v2, i2):
    assert PEER_TOPK == 16
    n = v1.shape[1]
    vals, ids = [], []
    for a in range(8):
        nb = PEER_TOPK if a == 0 else 8
        sv = v1[a:a + 1, :] + v2[:nb, :]
        limit = PEER_TOPK // (a + 1)
        if limit < nb:
            sv = jnp.where(lax.broadcasted_iota(jnp.int32, (nb, n), 0) < limit, sv, -jnp.inf)
        vals.append(sv)
        ids.append(i1[a:a + 1, :] * PEER_KEYS + i2[:nb, :])
    vals.append(v1[8:, :] + v2[0:1, :])
    ids.append(i1[8:, :] * PEER_KEYS + i2[0:1, :])
    return jnp.concatenate(vals, axis=0), jnp.concatenate(ids, axis=0)


def _route_kernel(h_ref, wq_ref, keys_ref, idx_ref, gate_ref):
    hb = h_ref[...].astype(BF16)
    idx_rows, gate_rows = [], []
    for h in range(PEER_HEADS):
        tops = []
        for side in range(2):
            c0 = h * PEER_QDIM + side * PEER_HALF
            qh = jnp.dot(hb, wq_ref[:, c0:c0 + PEER_HALF], preferred_element_type=F32).astype(BF16)
            s = lax.dot_general(keys_ref[h, side], qh, (((1,), (1,)), ((), ())),
                                preferred_element_type=F32)
            tops.append(_topk_rows(s, PEER_TOPK))
        (v1, i1), (v2, i2) = tops
        cand, cand_ids = _product_candidates(v1, i1, v2, i2)
        vs, ids = _topk_rows(cand, PEER_TOPK, payload=cand_ids)
        idx_rows.append(ids)
        ex = jnp.exp(vs - jnp.max(vs, axis=0, keepdims=True))
        gate_rows.append(ex / jnp.sum(ex, axis=0, keepdims=True))
    idx_ref[...] = (jnp.concatenate(idx_rows, axis=0) * SLAB_ROWS).T
    gate_ref[...] = jnp.concatenate(gate_rows, axis=0).T


def _route(h2, w_query, sub_keys, tm):
    T, D = h2.shape
    wq = w_query.astype(BF16)
    keys = sub_keys.astype(BF16)
    return pl.pallas_call(
        _route_kernel,
        grid=(T // tm,),
        in_specs=[pl.BlockSpec((tm, D), lambda i: (i, 0)),
                  pl.BlockSpec(wq.shape, lambda i: (0, 0)),
                  pl.BlockSpec(keys.shape, lambda i: (0, 0, 0, 0))],
        out_specs=[pl.BlockSpec((tm, N_PICKS), lambda i: (i, 0))] * 2,
        out_shape=[jax.ShapeDtypeStruct((T, N_PICKS), jnp.int32),
                   jax.ShapeDtypeStruct((T, N_PICKS), F32)],
        compiler_params=_cparams(("parallel",)),
        name="route",
    )(h2, wq, keys)


SLAB_ROWS = 4
FEATURE_ROWS = 8
TOKENS_PER_ITER = 4


def _pack_table(experts):
    n_experts, D = experts.shape
    assert D == FEATURE_ROWS * LANES
    bits = lax.bitcast_convert_type(experts.astype(BF16), jnp.uint16).astype(jnp.uint32)
    bits = bits.reshape(n_experts, SLAB_ROWS, 2, LANES)
    words = bits[:, :, 0, :] | (bits[:, :, 1, :] << 16)
    return lax.bitcast_convert_type(words, jnp.int32).reshape(n_experts * SLAB_ROWS, LANES)


def _token_expert_rows(tbl_ref, picks):
    pairs = []
    for m in range(N_PICKS // 2):
        words = jnp.concatenate(
            [tbl_ref[pl.ds(pl.multiple_of(picks[2 * m + j], SLAB_ROWS), SLAB_ROWS), :] for j in range(2)], axis=0)
        pairs.append(pltpu.bitcast(words, BF16))
    return jnp.concatenate(pairs, axis=0)


def _own_row_mask():
    shape = (FEATURE_ROWS, N_PICKS * FEATURE_ROWS)
    return lax.broadcasted_iota(jnp.int32, shape, 1) % FEATURE_ROWS == lax.broadcasted_iota(jnp.int32, shape, 0)


def _token_loop(n_tokens, one_token):
    def body(i, carry):
        for u in range(TOKENS_PER_ITER):
            one_token(i * TOKENS_PER_ITER + u)
        return carry

    lax.fori_loop(0, n_tokens // TOKENS_PER_ITER, body, 0)


def _peer_u_kernel(rows_ref, x_ref, tbl_ref, z_ref, y_scr):
    own = _own_row_mask()

    def token(t):
        rows = _token_expert_rows(tbl_ref, rows_ref.at[t])
        part = lax.dot_general(x_ref[t].astype(BF16), rows, (((1,), (1,)), ((), ())),
                               preferred_element_type=F32)
        y_scr[pl.ds(t, 1), :] = jnp.sum(jnp.where(own, part, 0.0), axis=0, keepdims=True)

    _token_loop(x_ref.shape[0], token)
    shape = (N_PICKS * FEATURE_ROWS, N_PICKS)
    group = (lax.broadcasted_iota(jnp.int32, shape, 0) // FEATURE_ROWS
             == lax.broadcasted_iota(jnp.int32, shape, 1)).astype(F32)
    z_ref[...] = jnp.dot(y_scr[...], group, preferred_element_type=F32, precision=lax.Precision.HIGHEST)


def _gelu_tanh(z):
    return 0.5 * z * (1.0 + jnp.tanh(math.sqrt(2.0 / math.pi) * (z + 0.044715 * (z * z * z))))


def _peer_v_kernel(rows_ref, z_ref, gate_ref, tbl_ref, o_ref, w_scr):
    a = (_gelu_tanh(z_ref[...]) * gate_ref[...]).astype(BF16)
    shape = (N_PICKS, N_PICKS * FEATURE_ROWS)
    spread = (lax.broadcasted_iota(jnp.int32, shape, 1) // FEATURE_ROWS
              == lax.broadcasted_iota(jnp.int32, shape, 0)).astype(BF16)
    w_scr[...] = jnp.dot(a, spread, preferred_element_type=F32)
    own = _own_row_mask()

    def token(t):
        rows = _token_expert_rows(tbl_ref, rows_ref.at[t])
        left = jnp.where(own, w_scr[pl.ds(t, 1), :], 0.0).astype(BF16)
        o_ref[t] = jnp.dot(left, rows, preferred_element_type=F32)

    _token_loop(o_ref.shape[0], token)


def _peer(h2, rows, gate, u_experts, v_experts, tb):
    T, D = h2.shape
    assert tb % TOKENS_PER_ITER == 0
    n_rows = u_experts.shape[0] * SLAB_ROWS
    rows_spec = pl.BlockSpec((tb, N_PICKS), lambda i: (i, 0), memory_space=pltpu.SMEM)
    picks_spec = pl.BlockSpec((tb, N_PICKS), lambda i: (i, 0))
    feat_spec = pl.BlockSpec((tb, FEATURE_ROWS, LANES), lambda i: (i, 0, 0))
    table_spec = pl.BlockSpec((n_rows, LANES), lambda i: (0, 0), pipeline_mode=pl.Buffered(1))
    wide = N_PICKS * FEATURE_ROWS
    z = pl.pallas_call(
        _peer_u_kernel,
        grid=(T // tb,),
        in_specs=[rows_spec, feat_spec, table_spec],
        out_specs=picks_spec,
        out_shape=jax.ShapeDtypeStruct((T, N_PICKS), F32),
        scratch_shapes=[pltpu.VMEM((tb, wide), F32)],
        compiler_params=_cparams(("arbitrary",)),
        name="peer_u",
    )(rows, h2.reshape(T, FEATURE_ROWS, LANES), _pack_table(u_experts))
    out = pl.pallas_call(
        _peer_v_kernel,
        grid=(T // tb,),
        in_specs=[rows_spec, picks_spec, picks_spec, table_spec],
        out_specs=feat_spec,
        out_shape=jax.ShapeDtypeStruct((T, FEATURE_ROWS, LANES), F32),
        scratch_shapes=[pltpu.VMEM((tb, wide), F32)],
        compiler_params=_cparams(("arbitrary",)),
        name="peer_v",
    )(rows, z, gate, _pack_table(v_experts))
    return out.reshape(T, D)


def _final_kernel(x1_ref, p_ref, mod_ref, g_ref, o_ref):
    o_ref[0] = _rms(x1_ref[0] + mod_ref[0, 5:6, :] * p_ref[0], g_ref[...])


def _final(x1, peer_out, mods, g, tm):
    B, S, D = x1.shape
    tok = pl.BlockSpec((1, tm, D), lambda b, i: (b, i, 0))
    return pl.pallas_call(
        _final_kernel,
        grid=(B, S // tm),
        in_specs=[tok, tok, pl.BlockSpec((1, 6, D), lambda b, i: (b, 0, 0)),
                  pl.BlockSpec((1, D), lambda b, i: (0, 0))],
        out_specs=tok,
        out_shape=jax.ShapeDtypeStruct((B, S, D), F32),
        compiler_params=_cparams(("parallel", "parallel")),
        name="final",
    )(x1, peer_out, mods, g)


def _token_tile(S):
    return min(512, S)


def kernel(x, c, w_ada, b_ada, norm1_g, w_in, sink_a, t5_table, rpb_b, out_norm_a, out_norm_b,
           w_out, norm2_g, w_query, sub_keys, u_experts, v_experts, final_g):
    B, S, D = x.shape
    assert w_ada.shape[0] == 1, "the last residual add is fused with the final norm: one layer only"
    tm = _token_tile(S)
    mods = _ada(c, w_ada[0], b_ada[0])
    qa, ka, va, qb, kb, vb = _in_proj(x, mods, norm1_g[0].reshape(1, D), w_in[0], tm)
    o_a = _win_attn(qa, ka, va, sink_a[0], t5_table)
    o_b = _nbr_attn(qb, kb, vb, rpb_b[0])
    x1, h2 = _out_proj(o_a, o_b, x, mods, out_norm_a[0].reshape(1, W_QA), out_norm_b[0].reshape(1, W_B),
                       w_out[0], norm2_g[0].reshape(1, D), tm)
    h2f = h2.reshape(B * S, D)
    rows, gate = _route(h2f, w_query[0], sub_keys[0], min(256, B * S))
    peer_out = _peer(h2f, rows, gate, u_experts[0], v_experts[0], min(64, B * S))
    return _final(x1, peer_out.reshape(B, S, D), mods, final_g.reshape(1, D), tm)
```

```python
import functools
import math

import numpy as np
import jax
import jax.numpy as jnp
from jax import lax
from jax.experimental import pallas as pl
from jax.experimental.pallas import tpu as pltpu

F32 = jnp.float32
BF16 = jnp.bfloat16

HEAD_DIM = 64
N_HEADS_A = 8
N_KV_A = 2
N_HEADS_B = 8
WINDOW = 128
WIN_BLOCK = 128
T5_BUCKETS = 32
T5_MAX_DIST = 128
GRID_W = 64
NA_ROWS = 8
NA_COLS = 16
PEER_HEADS = 8
PEER_KEYS = 128
PEER_QDIM = 256
PEER_HALF = 128
PEER_TOPK = 16
N_PICKS = PEER_HEADS * PEER_TOPK
EPS = 1e-6
NEG = -1e30

W_QA = N_HEADS_A * HEAD_DIM
W_KVA = N_KV_A * HEAD_DIM
W_B = N_HEADS_B * HEAD_DIM

LANES = 128
VMEM_LIMIT = 60 * 1024 * 1024


def _cparams(sem):
    return pltpu.CompilerParams(dimension_semantics=sem, vmem_limit_bytes=VMEM_LIMIT)


def _ada_kernel(c_ref, w_ref, b_ref, o_ref):
    c = c_ref[...]
    act = c * jax.nn.sigmoid(c)
    o_ref[...] = jnp.dot(act, w_ref[...], preferred_element_type=F32,
                         precision=lax.Precision.HIGHEST) + b_ref[...]


def _ada(c, w_ada, b_ada):
    B, D = c.shape
    n_out = w_ada.shape[1]
    rows = 8
    c_pad = jnp.zeros((rows, D), F32).at[:B].set(c)
    out = pl.pallas_call(
        _ada_kernel,
        grid=(n_out // D,),
        in_specs=[pl.BlockSpec((rows, D), lambda j: (0, 0)),
                  pl.BlockSpec((D, D), lambda j: (0, j)),
                  pl.BlockSpec((1, D), lambda j: (0, j))],
        out_specs=pl.BlockSpec((rows, D), lambda j: (0, j)),
        out_shape=jax.ShapeDtypeStruct((rows, n_out), F32),
        compiler_params=_cparams(("arbitrary",)),
        name="ada",
    )(c_pad, w_ada, b_ada.reshape(1, n_out))
    return out[:B].reshape(B, n_out // D, D)


def _rms(x, g):
    return x * lax.rsqrt(jnp.mean(x * x, axis=-1, keepdims=True) + EPS) * g


def _in_proj_kernel(x_ref, mod_ref, g_ref, wq_ref, wk_ref, wv_ref, wqb_ref, wkb_ref, wvb_ref,
                    qa_ref, ka_ref, va_ref, qb_ref, kb_ref, vb_ref):
    x = x_ref[0]
    h = _rms(x, g_ref[...]) * (1.0 + mod_ref[0, 1:2, :]) + mod_ref[0, 0:1, :]
    hb = h.astype(BF16)
    scale = HEAD_DIM ** -0.5
    qa_ref[0] = (jnp.dot(hb, wq_ref[...], preferred_element_type=F32) * scale).astype(BF16)
    ka_ref[0] = jnp.dot(hb, wk_ref[...], preferred_element_type=F32).astype(BF16)
    va_ref[0] = jnp.dot(hb, wv_ref[...], preferred_element_type=F32).astype(BF16)
    qb_ref[0] = (jnp.dot(hb, wqb_ref[...], preferred_element_type=F32) * scale).astype(BF16)
    kb_ref[0] = jnp.dot(hb, wkb_ref[...], preferred_element_type=F32).astype(BF16)
    vb_ref[0] = jnp.dot(hb, wvb_ref[...], preferred_element_type=F32).astype(BF16)


def _dup_kv_cols(w):
    a, b = w[:, :HEAD_DIM], w[:, HEAD_DIM:]
    return jnp.concatenate([a, a, b, b], axis=1)


def _in_proj(x, mods, g1, w_in, tm):
    B, S, D = x.shape
    wb = w_in.astype(BF16)
    o = 0
    wq = wb[:, o:o + W_QA]; o += W_QA
    wk = _dup_kv_cols(wb[:, o:o + W_KVA]); o += W_KVA
    wv = _dup_kv_cols(wb[:, o:o + W_KVA]); o += W_KVA
    wqb = wb[:, o:o + W_B]; o += W_B
    wkb = wb[:, o:o + W_B]; o += W_B
    wvb = wb[:, o:o + W_B]
    full = lambda a: pl.BlockSpec(a.shape, lambda b, i: (0, 0))
    tok = lambda w: pl.BlockSpec((1, tm, w), lambda b, i: (b, i, 0))
    widths = (W_QA, 2 * W_KVA, 2 * W_KVA, W_B, W_B, W_B)
    return pl.pallas_call(
        _in_proj_kernel,
        grid=(B, S // tm),
        in_specs=[tok(D), pl.BlockSpec((1, 6, D), lambda b, i: (b, 0, 0)), full(g1),
                  full(wq), full(wk), full(wv), full(wqb), full(wkb), full(wvb)],
        out_specs=[tok(w) for w in widths],
        out_shape=[jax.ShapeDtypeStruct((B, S, w), BF16) for w in widths],
        compiler_params=_cparams(("parallel", "parallel")),
        name="in_proj",
    )(x, mods, g1, wq, wk, wv, wqb, wkb, wvb)


def _t5_bucket_np(rel):
    half = T5_BUCKETS // 2
    max_exact = half // 2
    ret = np.where(rel > 0, half, 0)
    n = np.abs(rel)
    nf = np.maximum(n, 1).astype(np.float32)
    large = max_exact + (np.log(nf / np.float32(max_exact)) / np.float32(math.log(T5_MAX_DIST / max_exact))
                         * np.float32(half - max_exact)).astype(np.int32)
    large = np.minimum(large, half - 1)
    return ret + np.where(n < max_exact, n, large)


def _lookup_kernel(tab_ref, sel_ref, o_ref):
    sel = sel_ref[...]
    onehot = (lax.broadcasted_iota(jnp.int32, (tab_ref.shape[1], sel.shape[1]), 0) == sel).astype(F32)
    val = jnp.dot(tab_ref[...], onehot, preferred_element_type=F32, precision=lax.Precision.HIGHEST)
    o_ref[...] = jnp.where(sel >= 0, val, NEG)


def _lookup(table, sel):
    R, M = table.shape
    m_pad = -(-M // 8) * 8
    table = jnp.zeros((R, m_pad), F32).at[:, :M].set(table.astype(F32))
    n = sel.shape[0]
    tn = min(n, 4096)
    return pl.pallas_call(
        _lookup_kernel,
        grid=(n // tn,),
        in_specs=[pl.BlockSpec((R, m_pad), lambda j: (0, 0)), pl.BlockSpec((1, tn), lambda j: (0, j))],
        out_specs=pl.BlockSpec((R, tn), lambda j: (0, j)),
        out_shape=jax.ShapeDtypeStruct((R, n), F32),
        compiler_params=_cparams(("parallel",)),
        name="bias_lookup",
    )(table, jnp.asarray(sel.reshape(1, n), jnp.int32))


def _softmax_rows(s, extra=None):
    m = jnp.max(s, axis=-1, keepdims=True)
    if extra is not None:
        m = jnp.maximum(m, extra)
    e = jnp.exp(s - m)
    den = jnp.sum(e, axis=-1, keepdims=True)
    if extra is not None:
        den = den + jnp.exp(extra - m)
    return e / den


WIN_BLOCKS_PER_STEP = 4


def _win_attn_kernel(sink_ref, q_ref, kp_ref, kc_ref, kn_ref, vp_ref, vc_ref, vn_ref, bias_ref, o_ref, *, seq):
    step = pl.program_id(1)
    k = jnp.concatenate([kp_ref[0], kc_ref[0], kn_ref[0]], axis=0)
    v = jnp.concatenate([vp_ref[0], vc_ref[0], vn_ref[0]], axis=0)
    span = WIN_BLOCK + 2 * WINDOW
    col = lax.broadcasted_iota(jnp.int32, (WIN_BLOCK, span), 1)
    low = lax.broadcasted_iota(jnp.int32, (WIN_BLOCK, LANES), 1) < HEAD_DIM
    zero = jnp.zeros((WIN_BLOCK, LANES), BF16)
    for blk in range(WIN_BLOCKS_PER_STEP):
        rows = slice(blk * WIN_BLOCK, (blk + 1) * WIN_BLOCK)
        kpos = col + (step * WIN_BLOCKS_PER_STEP + blk - 1) * WIN_BLOCK
        valid = (kpos >= 0) & (kpos < seq)
        for pair in range(N_HEADS_A // 2):
            qp = q_ref[0, rows, pair * LANES:(pair + 1) * LANES]
            outs = []
            for par in range(2):
                h = 2 * pair + par
                grp = h // (N_HEADS_A // N_KV_A)
                kg = k[blk * WIN_BLOCK:blk * WIN_BLOCK + span, grp * LANES:(grp + 1) * LANES]
                vg = v[blk * WIN_BLOCK:blk * WIN_BLOCK + span, grp * LANES:(grp + 1) * LANES]
                qh = jnp.where(low if par == 0 else jnp.logical_not(low), qp, zero)
                s = lax.dot_general(qh, kg, (((1,), (1,)), ((), ())), preferred_element_type=F32)
                s = jnp.where(valid, s + bias_ref[h], NEG)
                p = _softmax_rows(s, extra=sink_ref[h]).astype(BF16)
                outs.append(jnp.dot(p, vg, preferred_element_type=F32))
            o_ref[0, rows, pair * LANES:(pair + 1) * LANES] = jnp.where(low, outs[0], outs[1])


def _win_attn(qa, ka, va, sink, t5_table):
    B, S, _ = qa.shape
    nb = S // WIN_BLOCK
    per = WIN_BLOCKS_PER_STEP
    assert nb % per == 0
    span = WIN_BLOCK + 2 * WINDOW
    rel = np.arange(span)[None, :] - WINDOW - np.arange(WIN_BLOCK)[:, None]
    bucket = np.where(np.abs(rel) <= WINDOW, _t5_bucket_np(rel), -1)
    bias = _lookup(t5_table.T, bucket.reshape(-1)).reshape(N_HEADS_A, WIN_BLOCK, span)
    qspec = pl.BlockSpec((1, per * WIN_BLOCK, W_QA), lambda b, i: (b, i, 0))
    cur = pl.BlockSpec((1, per * WIN_BLOCK, 2 * W_KVA), lambda b, i: (b, i, 0))
    edge = lambda off: pl.BlockSpec((1, WIN_BLOCK, 2 * W_KVA),
                                    lambda b, i: (b, jnp.clip(i * per + off, 0, nb - 1), 0))
    return pl.pallas_call(
        functools.partial(_win_attn_kernel, seq=S),
        grid=(B, nb // per),
        in_specs=[pl.BlockSpec(memory_space=pltpu.SMEM), qspec,
                  edge(-1), cur, edge(per), edge(-1), cur, edge(per),
                  pl.BlockSpec(bias.shape, lambda b, i: (0, 0, 0))],
        out_specs=qspec,
        out_shape=jax.ShapeDtypeStruct((B, S, W_QA), F32),
        compiler_params=_cparams(("parallel", "parallel")),
        name="win_attn",
    )(sink.astype(F32), qa, ka, ka, ka, va, va, va, bias)


NBR_ROWS_PER_STEP = 8


def _nbr_attn_kernel(q_ref, kp_ref, kc_ref, kn_ref, vp_ref, vc_ref, vn_ref, bias_ref, o_ref, k_scr, v_scr, *, n_rows):
    step = pl.program_id(1)
    blk = NBR_ROWS_PER_STEP * GRID_W
    for j, (kr, vr) in enumerate(((kp_ref, vp_ref), (kc_ref, vc_ref), (kn_ref, vn_ref))):
        k_scr[j * blk:(j + 1) * blk, :] = kr[0]
        v_scr[j * blk:(j + 1) * blk, :] = vr[0]
    low = lax.broadcasted_iota(jnp.int32, (GRID_W, LANES), 1) < HEAD_DIM
    zero = jnp.zeros((GRID_W, LANES), BF16)
    band = NA_ROWS * GRID_W

    def one_row(i, carry):
        r = step * NBR_ROWS_PER_STEP + i
        rs = jnp.clip(r - NA_ROWS // 2, 0, n_rows - NA_ROWS)
        off = pl.multiple_of((rs - (step - 1) * NBR_ROWS_PER_STEP) * GRID_W, GRID_W)
        d = r - rs
        q_rows = pl.ds(pl.multiple_of(i * GRID_W, GRID_W), GRID_W)
        for pair in range(N_HEADS_B // 2):
            sl = slice(pair * LANES, (pair + 1) * LANES)
            qp = q_ref[0, q_rows, sl]
            kp = k_scr[pl.ds(off, band), sl]
            vp = v_scr[pl.ds(off, band), sl]
            outs = []
            for par in range(2):
                h = 2 * pair + par
                qh = jnp.where(low if par == 0 else jnp.logical_not(low), qp, zero)
                s = lax.dot_general(qh, kp, (((1,), (1,)), ((), ())), preferred_element_type=F32)
                p = _softmax_rows(s + bias_ref[d, h]).astype(BF16)
                outs.append(jnp.dot(p, vp, preferred_element_type=F32))
            o_ref[0, q_rows, sl] = jnp.where(low, outs[0], outs[1])
        return carry

    lax.fori_loop(0, NBR_ROWS_PER_STEP, one_row, 0)


def _nbr_bias(rpb):
    cols = np.arange(GRID_W)
    col_start = np.clip(cols - NA_COLS // 2, 0, GRID_W - NA_COLS)
    kc = np.arange(GRID_W)
    inside = (kc[None, :] >= col_start[:, None]) & (kc[None, :] < col_start[:, None] + NA_COLS)
    col_off = np.where(inside, kc[None, :] - cols[:, None] + (NA_COLS - 1), -1)
    table = jnp.stack([rpb[:, NA_ROWS - 1 - d:2 * NA_ROWS - 1 - d, :] for d in range(NA_ROWS)])
    b = _lookup(table.reshape(NA_ROWS * N_HEADS_B * NA_ROWS, 2 * NA_COLS - 1), col_off.reshape(-1))
    b = b.reshape(NA_ROWS, N_HEADS_B, NA_ROWS, GRID_W, GRID_W).transpose(0, 1, 3, 2, 4)
    return b.reshape(NA_ROWS, N_HEADS_B, GRID_W, NA_ROWS * GRID_W)


def _nbr_attn(qb, kb, vb, rpb):
    B, S, _ = qb.shape
    rows = S // GRID_W
    assert rows >= NA_ROWS
    assert rows % NBR_ROWS_PER_STEP == 0 and NBR_ROWS_PER_STEP >= NA_ROWS
    bias = _nbr_bias(rpb)
    n_steps = rows // NBR_ROWS_PER_STEP
    blk = NBR_ROWS_PER_STEP * GRID_W
    spec = lambda off: pl.BlockSpec((1, blk, W_B), lambda b, i: (b, jnp.clip(i + off, 0, n_steps - 1), 0))
    return pl.pallas_call(
        functools.partial(_nbr_attn_kernel, n_rows=rows),
        grid=(B, n_steps),
        in_specs=[spec(0), spec(-1), spec(0), spec(1), spec(-1), spec(0), spec(1),
                  pl.BlockSpec(bias.shape, lambda b, i: (0, 0, 0, 0), pipeline_mode=pl.Buffered(1))],
        out_specs=spec(0),
        out_shape=jax.ShapeDtypeStruct((B, S, W_B), F32),
        scratch_shapes=[pltpu.VMEM((3 * blk, W_B), BF16)] * 2,
        compiler_params=_cparams(("parallel", "parallel")),
        name="nbr_attn",
    )(qb, kb, kb, kb, vb, vb, vb, bias)


def _out_proj_kernel(oa_ref, ob_ref, x_ref, mod_ref, ga_ref, gb_ref, wa_ref, wb_ref, g2_ref, x1_ref, h2_ref):
    na = _rms(oa_ref[0], ga_ref[...]).astype(BF16)
    nb = _rms(ob_ref[0], gb_ref[...]).astype(BF16)
    y = (jnp.dot(na, wa_ref[...], preferred_element_type=F32)
         + jnp.dot(nb, wb_ref[...], preferred_element_type=F32))
    x1 = x_ref[0] + mod_ref[0, 2:3, :] * y
    x1_ref[0] = x1
    h2_ref[0] = _rms(x1, g2_ref[...]) * (1.0 + mod_ref[0, 4:5, :]) + mod_ref[0, 3:4, :]


def _out_proj(o_a, o_b, x, mods, ga, gb, w_out, g2, tm):
    B, S, D = x.shape
    wb16 = w_out.astype(BF16)
    wa, wb = wb16[:W_QA], wb16[W_QA:]
    full = lambda a: pl.BlockSpec(a.shape, lambda b, i: (0, 0))
    tok = lambda w: pl.BlockSpec((1, tm, w), lambda b, i: (b, i, 0))
    return pl.pallas_call(
        _out_proj_kernel,
        grid=(B, S // tm),
        in_specs=[tok(W_QA), tok(W_B), tok(D), pl.BlockSpec((1, 6, D), lambda b, i: (b, 0, 0)),
                  full(ga), full(gb), full(wa), full(wb), full(g2)],
        out_specs=[tok(D), tok(D)],
        out_shape=[jax.ShapeDtypeStruct((B, S, D), F32)] * 2,
        compiler_params=_cparams(("parallel", "parallel")),
        name="out_proj",
    )(o_a, o_b, x, mods, ga, gb, wa, wb, g2)


def _topk_rows(s, k, payload=None):
    n_rows = s.shape[0]
    row = lax.broadcasted_iota(jnp.int32, s.shape, 0)
    vals, picked = [], []
    for _ in range(k):
        m = jnp.max(s, axis=0, keepdims=True)
        am = jnp.min(jnp.where(s == m, row, n_rows), axis=0, keepdims=True)
        hit = row == am
        vals.append(m)
        picked.append(am if payload is None else jnp.sum(jnp.where(hit, payload, 0), axis=0, keepdims=True))
        s = jnp.where(hit, -jnp.inf, s)
    return jnp.concatenate(vals, axis=0), jnp.concatenate(picked, axis=0)


def _product_candidates(v1, i1, v2, i2):
    assert PEER_TOPK == 16
    n = v1.shape[1]
    vals, ids = [], []
    for a in range(8):
        nb = PEER_TOPK if a == 0 else 8
        sv = v1[a:a + 1, :] + v2[:nb, :]
        limit = PEER_TOPK // (a + 1)
        if limit < nb:
            sv = jnp.where(lax.broadcasted_iota(jnp.int32, (nb, n), 0) < limit, sv, -jnp.inf)
        vals.append(sv)
        ids.append(i1[a:a + 1, :] * PEER_KEYS + i2[:nb, :])
    vals.append(v1[8:, :] + v2[0:1, :])
    ids.append(i1[8:, :] * PEER_KEYS + i2[0:1, :])
    return jnp.concatenate(vals, axis=0), jnp.concatenate(ids, axis=0)


def _route_kernel(h_ref, wq_ref, keys_ref, idx_ref, gate_ref):
    hb = h_ref[...].astype(BF16)
    idx_rows, gate_rows = [], []
    for h in range(PEER_HEADS):
        tops = []
        for side in range(2):
            c0 = h * PEER_QDIM + side * PEER_HALF
            qh = jnp.dot(hb, wq_ref[:, c0:c0 + PEER_HALF], preferred_element_type=F32).astype(BF16)
            s = lax.dot_general(keys_ref[h, side], qh, (((1,), (1,)), ((), ())),
                                preferred_element_type=F32)
            tops.append(_topk_rows(s, PEER_TOPK))
        (v1, i1), (v2, i2) = tops
        cand, cand_ids = _product_candidates(v1, i1, v2, i2)
        vs, ids = _topk_rows(cand, PEER_TOPK, payload=cand_ids)
        idx_rows.append(ids)
        ex = jnp.exp(vs - jnp.max(vs, axis=0, keepdims=True))
        gate_rows.append(ex / jnp.sum(ex, axis=0, keepdims=True))
    idx_ref[...] = (jnp.concatenate(idx_rows, axis=0) * SLAB_ROWS).T
    gate_ref[...] = jnp.concatenate(gate_rows, axis=0).T


def _route(h2, w_query, sub_keys, tm):
    T, D = h2.shape
    wq = w_query.astype(BF16)
    keys = sub_keys.astype(BF16)
    return pl.pallas_call(
        _route_kernel,
        grid=(T // tm,),
        in_specs=[pl.BlockSpec((tm, D), lambda i: (i, 0)),
                  pl.BlockSpec(wq.shape, lambda i: (0, 0)),
                  pl.BlockSpec(keys.shape, lambda i: (0, 0, 0, 0))],
        out_specs=[pl.BlockSpec((tm, N_PICKS), lambda i: (i, 0))] * 2,
        out_shape=[jax.ShapeDtypeStruct((T, N_PICKS), jnp.int32),
                   jax.ShapeDtypeStruct((T, N_PICKS), F32)],
        compiler_params=_cparams(("parallel",)),
        name="route",
    )(h2, wq, keys)


SLAB_ROWS = 4
FEATURE_ROWS = 8
TOKENS_PER_ITER = 16


def _pack_rows_kernel(x_ref, o_ref):
    o_ref[...] = pltpu.bitcast(x_ref[...], jnp.int32)


def _packed_expert_table(experts):
    n_experts, D = experts.shape
    assert D == FEATURE_ROWS * LANES
    rows = experts.astype(BF16).reshape(n_experts * FEATURE_ROWS, LANES)
    tile = 2048 * FEATURE_ROWS
    return pl.pallas_call(
        _pack_rows_kernel,
        grid=(rows.shape[0] // tile,),
        in_specs=[pl.BlockSpec((tile, LANES), lambda i: (i, 0))],
        out_specs=pl.BlockSpec((tile // 2, LANES), lambda i: (i, 0)),
        out_shape=jax.ShapeDtypeStruct((rows.shape[0] // 2, LANES), jnp.int32),
        compiler_params=_cparams(("parallel",)),
        name="pack_rows",
    )(rows)


def _token_expert_rows(tbl_ref, picks):
    pairs = []
    for m in range(N_PICKS // 2):
        words = jnp.concatenate(
            [tbl_ref[pl.ds(pl.multiple_of(picks[2 * m + j], SLAB_ROWS), SLAB_ROWS), :] for j in range(2)], axis=0)
        pairs.append(pltpu.bitcast(words, BF16))
    return jnp.concatenate(pairs, axis=0)


def _own_row_mask():
    shape = (FEATURE_ROWS, N_PICKS * FEATURE_ROWS)
    return lax.broadcasted_iota(jnp.int32, shape, 1) % FEATURE_ROWS == lax.broadcasted_iota(jnp.int32, shape, 0)


def _token_loop(n_tokens, one_token):
    def body(i, carry):
        for u in range(TOKENS_PER_ITER):
            one_token(i * TOKENS_PER_ITER + u)
        return carry

    lax.fori_loop(0, n_tokens // TOKENS_PER_ITER, body, 0)


def _peer_u_kernel(rows_ref, x_ref, tbl_ref, z_ref, y_scr):
    own = _own_row_mask()

    def token(t):
        rows = _token_expert_rows(tbl_ref, rows_ref.at[t])
        part = lax.dot_general(x_ref[t].astype(BF16), rows, (((1,), (1,)), ((), ())),
                               preferred_element_type=F32)
        y_scr[pl.ds(t, 1), :] = jnp.sum(jnp.where(own, part, 0.0), axis=0, keepdims=True)

    _token_loop(x_ref.shape[0], token)
    shape = (N_PICKS * FEATURE_ROWS, N_PICKS)
    group = (lax.broadcasted_iota(jnp.int32, shape, 0) // FEATURE_ROWS
             == lax.broadcasted_iota(jnp.int32, shape, 1)).astype(F32)
    z_ref[...] = jnp.dot(y_scr[...], group, preferred_element_type=F32, precision=lax.Precision.HIGHEST)


def _gelu_tanh(z):
    return 0.5 * z * (1.0 + jnp.tanh(math.sqrt(2.0 / math.pi) * (z + 0.044715 * (z * z * z))))


def _peer_v_kernel(rows_ref, z_ref, gate_ref, tbl_ref, o_ref, w_scr):
    a = (_gelu_tanh(z_ref[...]) * gate_ref[...]).astype(BF16)
    shape = (N_PICKS, N_PICKS * FEATURE_ROWS)
    spread = (lax.broadcasted_iota(jnp.int32, shape, 1) // FEATURE_ROWS
              == lax.broadcasted_iota(jnp.int32, shape, 0)).astype(BF16)
    w_scr[...] = jnp.dot(a, spread, preferred_element_type=F32)
    own = _own_row_mask()

    def token(t):
        rows = _token_expert_rows(tbl_ref, rows_ref.at[t])
        left = jnp.where(own, w_scr[pl.ds(t, 1), :], 0.0).astype(BF16)
        o_ref[t] = jnp.dot(left, rows, preferred_element_type=F32)

    _token_loop(o_ref.shape[0], token)


def _peer(h2, rows, gate, u_experts, v_experts, tb):
    T, D = h2.shape
    assert tb % TOKENS_PER_ITER == 0
    rows_spec = pl.BlockSpec((tb, N_PICKS), lambda i: (i, 0), memory_space=pltpu.SMEM)
    picks_spec = pl.BlockSpec((tb, N_PICKS), lambda i: (i, 0))
    feat_spec = pl.BlockSpec((tb, FEATURE_ROWS, LANES), lambda i: (i, 0, 0))
    table_spec = pl.BlockSpec((u_experts.shape[0] * SLAB_ROWS, LANES), lambda i: (0, 0),
                              pipeline_mode=pl.Buffered(1))
    wide = N_PICKS * FEATURE_ROWS
    z = pl.pallas_call(
        _peer_u_kernel,
        grid=(T // tb,),
        in_specs=[rows_spec, feat_spec, table_spec],
        out_specs=picks_spec,
        out_shape=jax.ShapeDtypeStruct((T, N_PICKS), F32),
        scratch_shapes=[pltpu.VMEM((tb, wide), F32)],
        compiler_params=_cparams(("arbitrary",)),
        name="peer_u",
    )(rows, h2.reshape(T, FEATURE_ROWS, LANES), _packed_expert_table(u_experts))
    out = pl.pallas_call(
        _peer_v_kernel,
        grid=(T // tb,),
        in_specs=[rows_spec, picks_spec, picks_spec, table_spec],
        out_specs=feat_spec,
        out_shape=jax.ShapeDtypeStruct((T, FEATURE_ROWS, LANES), F32),
        scratch_shapes=[pltpu.VMEM((tb, wide), F32)],
        compiler_params=_cparams(("arbitrary",)),
        name="peer_v",
    )(rows, z, gate, _packed_expert_table(v_experts))
    return out.reshape(T, D)


def _final_kernel(x1_ref, p_ref, mod_ref, g_ref, o_ref):
    o_ref[0] = _rms(x1_ref[0] + mod_ref[0, 5:6, :] * p_ref[0], g_ref[...])


def _final(x1, peer_out, mods, g, tm):
    B, S, D = x1.shape
    tok = pl.BlockSpec((1, tm, D), lambda b, i: (b, i, 0))
    return pl.pallas_call(
        _final_kernel,
        grid=(B, S // tm),
        in_specs=[tok, tok, pl.BlockSpec((1, 6, D), lambda b, i: (b, 0, 0)),
                  pl.BlockSpec((1, D), lambda b, i: (0, 0))],
        out_specs=tok,
        out_shape=jax.ShapeDtypeStruct((B, S, D), F32),
        compiler_params=_cparams(("parallel", "parallel")),
        name="final",
    )(x1, peer_out, mods, g)


def _token_tile(S):
    return min(512, S)


def kernel(x, c, w_ada, b_ada, norm1_g, w_in, sink_a, t5_table, rpb_b, out_norm_a, out_norm_b,
           w_out, norm2_g, w_query, sub_keys, u_experts, v_experts, final_g):
    B, S, D = x.shape
    assert w_ada.shape[0] == 1, "the last residual add is fused with the final norm: one layer only"
    tm = _token_tile(S)
    mods = _ada(c, w_ada[0], b_ada[0])
    qa, ka, va, qb, kb, vb = _in_proj(x, mods, norm1_g[0].reshape(1, D), w_in[0], tm)
    o_a = _win_attn(qa, ka, va, sink_a[0], t5_table)
    o_b = _nbr_attn(qb, kb, vb, rpb_b[0])
    x1, h2 = _out_proj(o_a, o_b, x, mods, out_norm_a[0].reshape(1, W_QA), out_norm_b[0].reshape(1, W_B),
                       w_out[0], norm2_g[0].reshape(1, D), tm)
    h2f = h2.reshape(B * S, D)
    rows, gate = _route(h2f, w_query[0], sub_keys[0], min(256, B * S))
    peer_out = _peer(h2f, rows, gate, u_experts[0], v_experts[0], min(64, B * S))
    return _final(x1, peer_out.reshape(B, S, D), mods, final_g.reshape(1, D), tm)
```

```python
import functools
import math

import numpy as np
import jax
import jax.numpy as jnp
from jax import lax
from jax.experimental import pallas as pl
from jax.experimental.pallas import tpu as pltpu

F32 = jnp.float32
BF16 = jnp.bfloat16

HEAD_DIM = 64
N_HEADS_A = 8
N_KV_A = 2
N_HEADS_B = 8
WINDOW = 128
WIN_BLOCK = 128
T5_BUCKETS = 32
T5_MAX_DIST = 128
GRID_W = 64
NA_ROWS = 8
NA_COLS = 16
PEER_HEADS = 8
PEER_KEYS = 128
PEER_QDIM = 256
PEER_HALF = 128
PEER_TOPK = 16
N_PICKS = PEER_HEADS * PEER_TOPK
EPS = 1e-6
NEG = -1e30

W_QA = N_HEADS_A * HEAD_DIM
W_KVA = N_KV_A * HEAD_DIM
W_B = N_HEADS_B * HEAD_DIM

LANES = 128
VMEM_LIMIT = 60 * 1024 * 1024


def _cparams(sem):
    return pltpu.CompilerParams(dimension_semantics=sem, vmem_limit_bytes=VMEM_LIMIT)


def _ada_kernel(c_ref, w_ref, b_ref, o_ref):
    c = c_ref[...]
    act = c * jax.nn.sigmoid(c)
    o_ref[...] = jnp.dot(act, w_ref[...], preferred_element_type=F32,
                         precision=lax.Precision.HIGHEST) + b_ref[...]


def _ada(c, w_ada, b_ada):
    B, D = c.shape
    n_out = w_ada.shape[1]
    rows = 8
    c_pad = jnp.zeros((rows, D), F32).at[:B].set(c)
    out = pl.pallas_call(
        _ada_kernel,
        grid=(n_out // D,),
        in_specs=[pl.BlockSpec((rows, D), lambda j: (0, 0)),
                  pl.BlockSpec((D, D), lambda j: (0, j)),
                  pl.BlockSpec((1, D), lambda j: (0, j))],
        out_specs=pl.BlockSpec((rows, D), lambda j: (0, j)),
        out_shape=jax.ShapeDtypeStruct((rows, n_out), F32),
        compiler_params=_cparams(("arbitrary",)),
        name="ada",
    )(c_pad, w_ada, b_ada.reshape(1, n_out))
    return out[:B].reshape(B, n_out // D, D)


def _rms(x, g):
    return x * lax.rsqrt(jnp.mean(x * x, axis=-1, keepdims=True) + EPS) * g


def _in_proj_kernel(x_ref, mod_ref, g_ref, wq_ref, wk_ref, wv_ref, wqb_ref, wkb_ref, wvb_ref,
                    qa_ref, ka_ref, va_ref, qb_ref, kb_ref, vb_ref):
    x = x_ref[0]
    h = _rms(x, g_ref[...]) * (1.0 + mod_ref[0, 1:2, :]) + mod_ref[0, 0:1, :]
    hb = h.astype(BF16)
    scale = HEAD_DIM ** -0.5
    qa_ref[0] = (jnp.dot(hb, wq_ref[...], preferred_element_type=F32) * scale).astype(BF16)
    ka_ref[0] = jnp.dot(hb, wk_ref[...], preferred_element_type=F32).astype(BF16)
    va_ref[0] = jnp.dot(hb, wv_ref[...], preferred_element_type=F32).astype(BF16)
    qb_ref[0] = (jnp.dot(hb, wqb_ref[...], preferred_element_type=F32) * scale).astype(BF16)
    kb_ref[0] = jnp.dot(hb, wkb_ref[...], preferred_element_type=F32).astype(BF16)
    vb_ref[0] = jnp.dot(hb, wvb_ref[...], preferred_element_type=F32).astype(BF16)


def _dup_kv_cols(w):
    a, b = w[:, :HEAD_DIM], w[:, HEAD_DIM:]
    return jnp.concatenate([a, a, b, b], axis=1)


def _in_proj(x, mods, g1, w_in, tm):
    B, S, D = x.shape
    wb = w_in.astype(BF16)
    o = 0
    wq = wb[:, o:o + W_QA]; o += W_QA
    wk = _dup_kv_cols(wb[:, o:o + W_KVA]); o += W_KVA
    wv = _dup_kv_cols(wb[:, o:o + W_KVA]); o += W_KVA
    wqb = wb[:, o:o + W_B]; o += W_B
    wkb = wb[:, o:o + W_B]; o += W_B
    wvb = wb[:, o:o + W_B]
    full = lambda a: pl.BlockSpec(a.shape, lambda b, i: (0, 0))
    tok = lambda w: pl.BlockSpec((1, tm, w), lambda b, i: (b, i, 0))
    widths = (W_QA, 2 * W_KVA, 2 * W_KVA, W_B, W_B, W_B)
    return pl.pallas_call(
        _in_proj_kernel,
        grid=(B, S // tm),
        in_specs=[tok(D), pl.BlockSpec((1, 6, D), lambda b, i: (b, 0, 0)), full(g1),
                  full(wq), full(wk), full(wv), full(wqb), full(wkb), full(wvb)],
        out_specs=[tok(w) for w in widths],
        out_shape=[jax.ShapeDtypeStruct((B, S, w), BF16) for w in widths],
        compiler_params=_cparams(("parallel", "parallel")),
        name="in_proj",
    )(x, mods, g1, wq, wk, wv, wqb, wkb, wvb)


def _t5_bucket_np(rel):
    half = T5_BUCKETS // 2
    max_exact = half // 2
    ret = np.where(rel > 0, half, 0)
    n = np.abs(rel)
    nf = np.maximum(n, 1).astype(np.float32)
    large = max_exact + (np.log(nf / np.float32(max_exact)) / np.float32(math.log(T5_MAX_DIST / max_exact))
                         * np.float32(half - max_exact)).astype(np.int32)
    large = np.minimum(large, half - 1)
    return ret + np.where(n < max_exact, n, large)


def _lookup_kernel(tab_ref, sel_ref, o_ref):
    sel = sel_ref[...]
    onehot = (lax.broadcasted_iota(jnp.int32, (tab_ref.shape[1], sel.shape[1]), 0) == sel).astype(F32)
    val = jnp.dot(tab_ref[...], onehot, preferred_element_type=F32, precision=lax.Precision.HIGHEST)
    o_ref[...] = jnp.where(sel >= 0, val, NEG)


def _lookup(table, sel):
    R, M = table.shape
    m_pad = -(-M // 8) * 8
    table = jnp.zeros((R, m_pad), F32).at[:, :M].set(table.astype(F32))
    n = sel.shape[0]
    tn = min(n, 4096)
    return pl.pallas_call(
        _lookup_kernel,
        grid=(n // tn,),
        in_specs=[pl.BlockSpec((R, m_pad), lambda j: (0, 0)), pl.BlockSpec((1, tn), lambda j: (0, j))],
        out_specs=pl.BlockSpec((R, tn), lambda j: (0, j)),
        out_shape=jax.ShapeDtypeStruct((R, n), F32),
        compiler_params=_cparams(("parallel",)),
        name="bias_lookup",
    )(table, jnp.asarray(sel.reshape(1, n), jnp.int32))


def _softmax_rows(s, extra=None):
    m = jnp.max(s, axis=-1, keepdims=True)
    if extra is not None:
        m = jnp.maximum(m, extra)
    e = jnp.exp(s - m)
    den = jnp.sum(e, axis=-1, keepdims=True)
    if extra is not None:
        den = den + jnp.exp(extra - m)
    return e / den


WIN_BLOCKS_PER_STEP = 4


def _win_attn_kernel(sink_ref, q_ref, kp_ref, kc_ref, kn_ref, vp_ref, vc_ref, vn_ref, bias_ref, o_ref,
                     s_scr, p_scr, *, seq):
    step = pl.program_id(1)
    k = jnp.concatenate([kp_ref[0], kc_ref[0], kn_ref[0]], axis=0)
    v = jnp.concatenate([vp_ref[0], vc_ref[0], vn_ref[0]], axis=0)
    span = WIN_BLOCK + 2 * WINDOW
    col = lax.broadcasted_iota(jnp.int32, (WIN_BLOCK, span), 1)
    low = lax.broadcasted_iota(jnp.int32, (WIN_BLOCK, LANES), 1) < HEAD_DIM
    zero = jnp.zeros((WIN_BLOCK, LANES), BF16)
    group = lambda a, blk, h: a[blk * WIN_BLOCK:blk * WIN_BLOCK + span,
                                (h // (N_HEADS_A // N_KV_A)) * LANES:(h // (N_HEADS_A // N_KV_A) + 1) * LANES]
    for blk in range(WIN_BLOCKS_PER_STEP):
        rows = slice(blk * WIN_BLOCK, (blk + 1) * WIN_BLOCK)
        kpos = col + (step * WIN_BLOCKS_PER_STEP + blk - 1) * WIN_BLOCK
        valid = (kpos >= 0) & (kpos < seq)
        for h in range(N_HEADS_A):
            qp = q_ref[0, rows, (h // 2) * LANES:(h // 2 + 1) * LANES]
            qh = jnp.where(low if h % 2 == 0 else jnp.logical_not(low), qp, zero)
            s_scr[h] = lax.dot_general(qh, group(k, blk, h), (((1,), (1,)), ((), ())),
                                       preferred_element_type=F32)
        for h in range(N_HEADS_A):
            s = jnp.where(valid, s_scr[h] + bias_ref[h], NEG)
            p_scr[h] = _softmax_rows(s, extra=sink_ref[h]).astype(BF16)
        for pair in range(N_HEADS_A // 2):
            outs = [jnp.dot(p_scr[2 * pair + par], group(v, blk, 2 * pair + par), preferred_element_type=F32)
                    for par in range(2)]
            o_ref[0, rows, pair * LANES:(pair + 1) * LANES] = jnp.where(low, outs[0], outs[1])


def _win_attn(qa, ka, va, sink, t5_table):
    B, S, _ = qa.shape
    nb = S // WIN_BLOCK
    per = WIN_BLOCKS_PER_STEP
    assert nb % per == 0
    span = WIN_BLOCK + 2 * WINDOW
    rel = np.arange(span)[None, :] - WINDOW - np.arange(WIN_BLOCK)[:, None]
    bucket = np.where(np.abs(rel) <= WINDOW, _t5_bucket_np(rel), -1)
    bias = _lookup(t5_table.T, bucket.reshape(-1)).reshape(N_HEADS_A, WIN_BLOCK, span)
    qspec = pl.BlockSpec((1, per * WIN_BLOCK, W_QA), lambda b, i: (b, i, 0))
    cur = pl.BlockSpec((1, per * WIN_BLOCK, 2 * W_KVA), lambda b, i: (b, i, 0))
    edge = lambda off: pl.BlockSpec((1, WIN_BLOCK, 2 * W_KVA),
                                    lambda b, i: (b, jnp.clip(i * per + off, 0, nb - 1), 0))
    return pl.pallas_call(
        functools.partial(_win_attn_kernel, seq=S),
        grid=(B, nb // per),
        in_specs=[pl.BlockSpec(memory_space=pltpu.SMEM), qspec,
                  edge(-1), cur, edge(per), edge(-1), cur, edge(per),
                  pl.BlockSpec(bias.shape, lambda b, i: (0, 0, 0))],
        out_specs=qspec,
        out_shape=jax.ShapeDtypeStruct((B, S, W_QA), F32),
        scratch_shapes=[pltpu.VMEM((N_HEADS_A, WIN_BLOCK, span), F32),
                        pltpu.VMEM((N_HEADS_A, WIN_BLOCK, span), BF16)],
        compiler_params=_cparams(("parallel", "parallel")),
        name="win_attn",
    )(sink.astype(F32), qa, ka, ka, ka, va, va, va, bias)


NBR_ROWS_PER_STEP = 8


def _nbr_attn_kernel(q_ref, kp_ref, kc_ref, kn_ref, vp_ref, vc_ref, vn_ref, bias_ref, o_ref,
                     k_scr, v_scr, s_scr, p_scr, *, n_rows):
    step = pl.program_id(1)
    blk = NBR_ROWS_PER_STEP * GRID_W
    for j, (kr, vr) in enumerate(((kp_ref, vp_ref), (kc_ref, vc_ref), (kn_ref, vn_ref))):
        k_scr[j * blk:(j + 1) * blk, :] = kr[0]
        v_scr[j * blk:(j + 1) * blk, :] = vr[0]
    low = lax.broadcasted_iota(jnp.int32, (GRID_W, LANES), 1) < HEAD_DIM
    zero = jnp.zeros((GRID_W, LANES), BF16)
    band = NA_ROWS * GRID_W

    def one_row(i, carry):
        r = step * NBR_ROWS_PER_STEP + i
        rs = jnp.clip(r - NA_ROWS // 2, 0, n_rows - NA_ROWS)
        off = pl.multiple_of((rs - (step - 1) * NBR_ROWS_PER_STEP) * GRID_W, GRID_W)
        d = r - rs
        q_rows = pl.ds(pl.multiple_of(i * GRID_W, GRID_W), GRID_W)
        pair_lanes = lambda h: slice((h // 2) * LANES, (h // 2 + 1) * LANES)
        for h in range(N_HEADS_B):
            qp = q_ref[0, q_rows, pair_lanes(h)]
            qh = jnp.where(low if h % 2 == 0 else jnp.logical_not(low), qp, zero)
            s_scr[h] = lax.dot_general(qh, k_scr[pl.ds(off, band), pair_lanes(h)], (((1,), (1,)), ((), ())),
                                       preferred_element_type=F32)
        for h in range(N_HEADS_B):
            p_scr[h] = _softmax_rows(s_scr[h] + bias_ref[d, h]).astype(BF16)
        for pair in range(N_HEADS_B // 2):
            sl = slice(pair * LANES, (pair + 1) * LANES)
            vp = v_scr[pl.ds(off, band), sl]
            outs = [jnp.dot(p_scr[2 * pair + par], vp, preferred_element_type=F32) for par in range(2)]
            o_ref[0, q_rows, sl] = jnp.where(low, outs[0], outs[1])
        return carry

    lax.fori_loop(0, NBR_ROWS_PER_STEP, one_row, 0)


def _nbr_bias(rpb):
    cols = np.arange(GRID_W)
    col_start = np.clip(cols - NA_COLS // 2, 0, GRID_W - NA_COLS)
    kc = np.arange(GRID_W)
    inside = (kc[None, :] >= col_start[:, None]) & (kc[None, :] < col_start[:, None] + NA_COLS)
    col_off = np.where(inside, kc[None, :] - cols[:, None] + (NA_COLS - 1), -1)
    table = jnp.stack([rpb[:, NA_ROWS - 1 - d:2 * NA_ROWS - 1 - d, :] for d in range(NA_ROWS)])
    b = _lookup(table.reshape(NA_ROWS * N_HEADS_B * NA_ROWS, 2 * NA_COLS - 1), col_off.reshape(-1))
    b = b.reshape(NA_ROWS, N_HEADS_B, NA_ROWS, GRID_W, GRID_W).transpose(0, 1, 3, 2, 4)
    return b.reshape(NA_ROWS, N_HEADS_B, GRID_W, NA_ROWS * GRID_W)


def _nbr_attn(qb, kb, vb, rpb):
    B, S, _ = qb.shape
    rows = S // GRID_W
    assert rows >= NA_ROWS
    assert rows % NBR_ROWS_PER_STEP == 0 and NBR_ROWS_PER_STEP >= NA_ROWS
    bias = _nbr_bias(rpb)
    n_steps = rows // NBR_ROWS_PER_STEP
    blk = NBR_ROWS_PER_STEP * GRID_W
    spec = lambda off: pl.BlockSpec((1, blk, W_B), lambda b, i: (b, jnp.clip(i + off, 0, n_steps - 1), 0))
    return pl.pallas_call(
        functools.partial(_nbr_attn_kernel, n_rows=rows),
        grid=(B, n_steps),
        in_specs=[spec(0), spec(-1), spec(0), spec(1), spec(-1), spec(0), spec(1),
                  pl.BlockSpec(bias.shape, lambda b, i: (0, 0, 0, 0), pipeline_mode=pl.Buffered(1))],
        out_specs=spec(0),
        out_shape=jax.ShapeDtypeStruct((B, S, W_B), F32),
        scratch_shapes=[pltpu.VMEM((3 * blk, W_B), BF16)] * 2
                       + [pltpu.VMEM((N_HEADS_B, GRID_W, NA_ROWS * GRID_W), F32),
                          pltpu.VMEM((N_HEADS_B, GRID_W, NA_ROWS * GRID_W), BF16)],
        compiler_params=_cparams(("parallel", "parallel")),
        name="nbr_attn",
    )(qb, kb, kb, kb, vb, vb, vb, bias)


def _out_proj_kernel(oa_ref, ob_ref, x_ref, mod_ref, ga_ref, gb_ref, wa_ref, wb_ref, g2_ref, x1_ref, h2_ref):
    na = _rms(oa_ref[0], ga_ref[...]).astype(BF16)
    nb = _rms(ob_ref[0], gb_ref[...]).astype(BF16)
    y = (jnp.dot(na, wa_ref[...], preferred_element_type=F32)
         + jnp.dot(nb, wb_ref[...], preferred_element_type=F32))
    x1 = x_ref[0] + mod_ref[0, 2:3, :] * y
    x1_ref[0] = x1
    h2_ref[0] = _rms(x1, g2_ref[...]) * (1.0 + mod_ref[0, 4:5, :]) + mod_ref[0, 3:4, :]


def _out_proj(o_a, o_b, x, mods, ga, gb, w_out, g2, tm):
    B, S, D = x.shape
    wb16 = w_out.astype(BF16)
    wa, wb = wb16[:W_QA], wb16[W_QA:]
    full = lambda a: pl.BlockSpec(a.shape, lambda b, i: (0, 0))
    tok = lambda w: pl.BlockSpec((1, tm, w), lambda b, i: (b, i, 0))
    return pl.pallas_call(
        _out_proj_kernel,
        grid=(B, S // tm),
        in_specs=[tok(W_QA), tok(W_B), tok(D), pl.BlockSpec((1, 6, D), lambda b, i: (b, 0, 0)),
                  full(ga), full(gb), full(wa), full(wb), full(g2)],
        out_specs=[tok(D), tok(D)],
        out_shape=[jax.ShapeDtypeStruct((B, S, D), F32)] * 2,
        compiler_params=_cparams(("parallel", "parallel")),
        name="out_proj",
    )(o_a, o_b, x, mods, ga, gb, wa, wb, g2)


def _topk_rows(s, k, payload=None):
    n_rows = s.shape[0]
    row = lax.broadcasted_iota(jnp.int32, s.shape, 0)
    vals, picked = [], []
    for _ in range(k):
        m = jnp.max(s, axis=0, keepdims=True)
        am = jnp.min(jnp.where(s == m, row, n_rows), axis=0, keepdims=True)
        hit = row == am
        vals.append(m)
        picked.append(am if payload is None else jnp.sum(jnp.where(hit, payload, 0), axis=0, keepdims=True))
        s = jnp.where(hit, -jnp.inf, s)
    return jnp.concatenate(vals, axis=0), jnp.concatenate(picked, axis=0)


def _product_candidates(v1, i1, v2, i2):
    assert PEER_TOPK == 16
    n = v1.shape[1]
    vals, ids = [], []
    for a in range(8):
        nb = PEER_TOPK if a == 0 else 8
        sv = v1[a:a + 1, :] + v2[:nb, :]
        limit = PEER_TOPK // (a + 1)
        if limit < nb:
            sv = jnp.where(lax.broadcasted_iota(jnp.int32, (nb, n), 0) < limit, sv, -jnp.inf)
        vals.append(sv)
        ids.append(i1[a:a + 1, :] * PEER_KEYS + i2[:nb, :])
    vals.append(v1[8:, :] + v2[0:1, :])
    ids.append(i1[8:, :] * PEER_KEYS + i2[0:1, :])
    return jnp.concatenate(vals, axis=0), jnp.concatenate(ids, axis=0)


def _route_kernel(h_ref, wq_ref, keys_ref, idx_ref, gate_ref):
    hb = h_ref[...].astype(BF16)
    idx_rows, gate_rows = [], []
    for h in range(PEER_HEADS):
        tops = []
        for side in range(2):
            c0 = h * PEER_QDIM + side * PEER_HALF
            qh = jnp.dot(hb, wq_ref[:, c0:c0 + PEER_HALF], preferred_element_type=F32).astype(BF16)
            s = lax.dot_general(keys_ref[h, side], qh, (((1,), (1,)), ((), ())),
                                preferred_element_type=F32)
            tops.append(_topk_rows(s, PEER_TOPK))
        (v1, i1), (v2, i2) = tops
        cand, cand_ids = _product_candidates(v1, i1, v2, i2)
        vs, ids = _topk_rows(cand, PEER_TOPK, payload=cand_ids)
        idx_rows.append(ids)
        ex = jnp.exp(vs - jnp.max(vs, axis=0, keepdims=True))
        gate_rows.append(ex / jnp.sum(ex, axis=0, keepdims=True))
    idx_ref[...] = (jnp.concatenate(idx_rows, axis=0) * SLAB_ROWS).T
    gate_ref[...] = jnp.concatenate(gate_rows, axis=0).T


def _route(h2, w_query, sub_keys, tm):
    T, D = h2.shape
    wq = w_query.astype(BF16)
    keys = sub_keys.astype(BF16)
    return pl.pallas_call(
        _route_kernel,
        grid=(T // tm,),
        in_specs=[pl.BlockSpec((tm, D), lambda i: (i, 0)),
                  pl.BlockSpec(wq.shape, lambda i: (0, 0)),
                  pl.BlockSpec(keys.shape, lambda i: (0, 0, 0, 0))],
        out_specs=[pl.BlockSpec((tm, N_PICKS), lambda i: (i, 0))] * 2,
        out_shape=[jax.ShapeDtypeStruct((T, N_PICKS), jnp.int32),
                   jax.ShapeDtypeStruct((T, N_PICKS), F32)],
        compiler_params=_cparams(("parallel",)),
        name="route",
    )(h2, wq, keys)


SLAB_ROWS = 4
FEATURE_ROWS = 8
TOKENS_PER_ITER = 16


def _pack_rows_kernel(x_ref, o_ref):
    o_ref[...] = pltpu.bitcast(x_ref[...], jnp.int32)


def _packed_expert_table(experts):
    n_experts, D = experts.shape
    assert D == FEATURE_ROWS * LANES
    rows = experts.astype(BF16).reshape(n_experts * FEATURE_ROWS, LANES)
    tile = 2048 * FEATURE_ROWS
    return pl.pallas_call(
        _pack_rows_kernel,
        grid=(rows.shape[0] // tile,),
        in_specs=[pl.BlockSpec((tile, LANES), lambda i: (i, 0))],
        out_specs=pl.BlockSpec((tile // 2, LANES), lambda i: (i, 0)),
        out_shape=jax.ShapeDtypeStruct((rows.shape[0] // 2, LANES), jnp.int32),
        compiler_params=_cparams(("parallel",)),
        name="pack_rows",
    )(rows)


def _token_expert_rows(tbl_ref, picks):
    pairs = []
    for m in range(N_PICKS // 2):
        words = jnp.concatenate(
            [tbl_ref[pl.ds(pl.multiple_of(picks[2 * m + j], SLAB_ROWS), SLAB_ROWS), :] for j in range(2)], axis=0)
        pairs.append(pltpu.bitcast(words, BF16))
    return jnp.concatenate(pairs, axis=0)


def _own_row_mask():
    shape = (FEATURE_ROWS, N_PICKS * FEATURE_ROWS)
    return lax.broadcasted_iota(jnp.int32, shape, 1) % FEATURE_ROWS == lax.broadcasted_iota(jnp.int32, shape, 0)


def _token_loop(n_tokens, one_token):
    def body(i, carry):
        for u in range(TOKENS_PER_ITER):
            one_token(i * TOKENS_PER_ITER + u)
        return carry

    lax.fori_loop(0, n_tokens // TOKENS_PER_ITER, body, 0)


def _peer_u_kernel(rows_ref, x_ref, tbl_ref, z_ref, y_scr):
    own = _own_row_mask()

    def token(t):
        rows = _token_expert_rows(tbl_ref, rows_ref.at[t])
        part = lax.dot_general(x_ref[t].astype(BF16), rows, (((1,), (1,)), ((), ())),
                               preferred_element_type=F32)
        y_scr[pl.ds(t, 1), :] = jnp.sum(jnp.where(own, part, 0.0), axis=0, keepdims=True)

    _token_loop(x_ref.shape[0], token)
    shape = (N_PICKS * FEATURE_ROWS, N_PICKS)
    group = (lax.broadcasted_iota(jnp.int32, shape, 0) // FEATURE_ROWS
             == lax.broadcasted_iota(jnp.int32, shape, 1)).astype(F32)
    z_ref[...] = jnp.dot(y_scr[...], group, preferred_element_type=F32, precision=lax.Precision.HIGHEST)


def _gelu_tanh(z):
    return 0.5 * z * (1.0 + jnp.tanh(math.sqrt(2.0 / math.pi) * (z + 0.044715 * (z * z * z))))


def _peer_v_kernel(rows_ref, z_ref, gate_ref, tbl_ref, o_ref, w_scr):
    a = (_gelu_tanh(z_ref[...]) * gate_ref[...]).astype(BF16)
    shape = (N_PICKS, N_PICKS * FEATURE_ROWS)
    spread = (lax.broadcasted_iota(jnp.int32, shape, 1) // FEATURE_ROWS
              == lax.broadcasted_iota(jnp.int32, shape, 0)).astype(BF16)
    w_scr[...] = jnp.dot(a, spread, preferred_element_type=F32)
    own = _own_row_mask()

    def token(t):
        rows = _token_expert_rows(tbl_ref, rows_ref.at[t])
        left = jnp.where(own, w_scr[pl.ds(t, 1), :], 0.0).astype(BF16)
        o_ref[t] = jnp.dot(left, rows, preferred_element_type=F32)

    _token_loop(o_ref.shape[0], token)


def _peer(h2, rows, gate, u_experts, v_experts, tb):
    T, D = h2.shape
    assert tb % TOKENS_PER_ITER == 0
    rows_spec = pl.BlockSpec((tb, N_PICKS), lambda i: (i, 0), memory_space=pltpu.SMEM)
    picks_spec = pl.BlockSpec((tb, N_PICKS), lambda i: (i, 0))
    feat_spec = pl.BlockSpec((tb, FEATURE_ROWS, LANES), lambda i: (i, 0, 0))
    table_spec = pl.BlockSpec((u_experts.shape[0] * SLAB_ROWS, LANES), lambda i: (0, 0),
                              pipeline_mode=pl.Buffered(1))
    wide = N_PICKS * FEATURE_ROWS
    z = pl.pallas_call(
        _peer_u_kernel,
        grid=(T // tb,),
        in_specs=[rows_spec, feat_spec, table_spec],
        out_specs=picks_spec,
        out_shape=jax.ShapeDtypeStruct((T, N_PICKS), F32),
        scratch_shapes=[pltpu.VMEM((tb, wide), F32)],
        compiler_params=_cparams(("arbitrary",)),
        name="peer_u",
    )(rows, h2.reshape(T, FEATURE_ROWS, LANES), _packed_expert_table(u_experts))
    out = pl.pallas_call(
        _peer_v_kernel,
        grid=(T // tb,),
        in_specs=[rows_spec, picks_spec, picks_spec, table_spec],
        out_specs=feat_spec,
        out_shape=jax.ShapeDtypeStruct((T, FEATURE_ROWS, LANES), F32),
        scratch_shapes=[pltpu.VMEM((tb, wide), F32)],
        compiler_params=_cparams(("arbitrary",)),
        name="peer_v",
    )(rows, z, gate, _packed_expert_table(v_experts))
    return out.reshape(T, D)


def _final_kernel(x1_ref, p_ref, mod_ref, g_ref, o_ref):
    o_ref[0] = _rms(x1_ref[0] + mod_ref[0, 5:6, :] * p_ref[0], g_ref[...])


def _final(x1, peer_out, mods, g, tm):
    B, S, D = x1.shape
    tok = pl.BlockSpec((1, tm, D), lambda b, i: (b, i, 0))
    return pl.pallas_call(
        _final_kernel,
        grid=(B, S // tm),
        in_specs=[tok, tok, pl.BlockSpec((1, 6, D), lambda b, i: (b, 0, 0)),
                  pl.BlockSpec((1, D), lambda b, i: (0, 0))],
        out_specs=tok,
        out_shape=jax.ShapeDtypeStruct((B, S, D), F32),
        compiler_params=_cparams(("parallel", "parallel")),
        name="final",
    )(x1, peer_out, mods, g)


def _token_tile(S):
    return min(512, S)


def kernel(x, c, w_ada, b_ada, norm1_g, w_in, sink_a, t5_table, rpb_b, out_norm_a, out_norm_b,
           w_out, norm2_g, w_query, sub_keys, u_experts, v_experts, final_g):
    B, S, D = x.shape
    assert w_ada.shape[0] == 1, "the last residual add is fused with the final norm: one layer only"
    tm = _token_tile(S)
    mods = _ada(c, w_ada[0], b_ada[0])
    qa, ka, va, qb, kb, vb = _in_proj(x, mods, norm1_g[0].reshape(1, D), w_in[0], tm)
    o_a = _win_attn(qa, ka, va, sink_a[0], t5_table)
    o_b = _nbr_attn(qb, kb, vb, rpb_b[0])
    x1, h2 = _out_proj(o_a, o_b, x, mods, out_norm_a[0].reshape(1, W_QA), out_norm_b[0].reshape(1, W_B),
                       w_out[0], norm2_g[0].reshape(1, D), tm)
    h2f = h2.reshape(B * S, D)
    rows, gate = _route(h2f, w_query[0], sub_keys[0], min(256, B * S))
    peer_out = _peer(h2f, rows, gate, u_experts[0], v_experts[0], min(64, B * S))
    return _final(x1, peer_out.reshape(B, S, D), mods, final_g.reshape(1, D), tm)
```

```python
import functools
import math

import numpy as np
import jax
import jax.numpy as jnp
from jax import lax
from jax.experimental import pallas as pl
from jax.experimental.pallas import tpu as pltpu

F32 = jnp.float32
BF16 = jnp.bfloat16

HEAD_DIM = 64
N_HEADS_A = 8
N_KV_A = 2
N_HEADS_B = 8
WINDOW = 128
WIN_BLOCK = 128
T5_BUCKETS = 32
T5_MAX_DIST = 128
GRID_W = 64
NA_ROWS = 8
NA_COLS = 16
PEER_HEADS = 8
PEER_KEYS = 128
PEER_QDIM = 256
PEER_HALF = 128
PEER_TOPK = 16
N_PICKS = PEER_HEADS * PEER_TOPK
EPS = 1e-6
NEG = -1e30

W_QA = N_HEADS_A * HEAD_DIM
W_KVA = N_KV_A * HEAD_DIM
W_B = N_HEADS_B * HEAD_DIM

LANES = 128
VMEM_LIMIT = 60 * 1024 * 1024


def _cparams(sem):
    return pltpu.CompilerParams(dimension_semantics=sem, vmem_limit_bytes=VMEM_LIMIT)


def _ada_kernel(c_ref, w_ref, b_ref, o_ref):
    c = c_ref[...]
    act = c * jax.nn.sigmoid(c)
    o_ref[...] = jnp.dot(act, w_ref[...], preferred_element_type=F32,
                         precision=lax.Precision.HIGHEST) + b_ref[...]


def _ada(c, w_ada, b_ada):
    B, D = c.shape
    n_out = w_ada.shape[1]
    rows = 8
    c_pad = jnp.zeros((rows, D), F32).at[:B].set(c)
    out = pl.pallas_call(
        _ada_kernel,
        grid=(n_out // D,),
        in_specs=[pl.BlockSpec((rows, D), lambda j: (0, 0)),
                  pl.BlockSpec((D, D), lambda j: (0, j)),
                  pl.BlockSpec((1, D), lambda j: (0, j))],
        out_specs=pl.BlockSpec((rows, D), lambda j: (0, j)),
        out_shape=jax.ShapeDtypeStruct((rows, n_out), F32),
        compiler_params=_cparams(("arbitrary",)),
        name="ada",
    )(c_pad, w_ada, b_ada.reshape(1, n_out))
    return out[:B].reshape(B, n_out // D, D)


def _rms(x, g):
    return x * lax.rsqrt(jnp.mean(x * x, axis=-1, keepdims=True) + EPS) * g


def _in_proj_kernel(x_ref, mod_ref, g_ref, wq_ref, wk_ref, wv_ref, wqb_ref, wkb_ref, wvb_ref,
                    qa_ref, ka_ref, va_ref, qb_ref, kb_ref, vb_ref):
    x = x_ref[0]
    h = _rms(x, g_ref[...]) * (1.0 + mod_ref[0, 1:2, :]) + mod_ref[0, 0:1, :]
    hb = h.astype(BF16)
    scale = HEAD_DIM ** -0.5
    qa_ref[0] = (jnp.dot(hb, wq_ref[...], preferred_element_type=F32) * scale).astype(BF16)
    ka_ref[0] = jnp.dot(hb, wk_ref[...], preferred_element_type=F32).astype(BF16)
    va_ref[0] = jnp.dot(hb, wv_ref[...], preferred_element_type=F32).astype(BF16)
    qb_ref[0] = (jnp.dot(hb, wqb_ref[...], preferred_element_type=F32) * scale).astype(BF16)
    kb_ref[0] = jnp.dot(hb, wkb_ref[...], preferred_element_type=F32).astype(BF16)
    vb_ref[0] = jnp.dot(hb, wvb_ref[...], preferred_element_type=F32).astype(BF16)


def _dup_kv_cols(w):
    a, b = w[:, :HEAD_DIM], w[:, HEAD_DIM:]
    return jnp.concatenate([a, a, b, b], axis=1)


def _in_proj(x, mods, g1, w_in, tm):
    B, S, D = x.shape
    wb = w_in.astype(BF16)
    o = 0
    wq = wb[:, o:o + W_QA]; o += W_QA
    wk = _dup_kv_cols(wb[:, o:o + W_KVA]); o += W_KVA
    wv = _dup_kv_cols(wb[:, o:o + W_KVA]); o += W_KVA
    wqb = wb[:, o:o + W_B]; o += W_B
    wkb = wb[:, o:o + W_B]; o += W_B
    wvb = wb[:, o:o + W_B]
    full = lambda a: pl.BlockSpec(a.shape, lambda b, i: (0, 0))
    tok = lambda w: pl.BlockSpec((1, tm, w), lambda b, i: (b, i, 0))
    widths = (W_QA, 2 * W_KVA, 2 * W_KVA, W_B, W_B, W_B)
    return pl.pallas_call(
        _in_proj_kernel,
        grid=(B, S // tm),
        in_specs=[tok(D), pl.BlockSpec((1, 6, D), lambda b, i: (b, 0, 0)), full(g1),
                  full(wq), full(wk), full(wv), full(wqb), full(wkb), full(wvb)],
        out_specs=[tok(w) for w in widths],
        out_shape=[jax.ShapeDtypeStruct((B, S, w), BF16) for w in widths],
        compiler_params=_cparams(("parallel", "parallel")),
        name="in_proj",
    )(x, mods, g1, wq, wk, wv, wqb, wkb, wvb)


def _t5_bucket_np(rel):
    half = T5_BUCKETS // 2
    max_exact = half // 2
    ret = np.where(rel > 0, half, 0)
    n = np.abs(rel)
    nf = np.maximum(n, 1).astype(np.float32)
    large = max_exact + (np.log(nf / np.float32(max_exact)) / np.float32(math.log(T5_MAX_DIST / max_exact))
                         * np.float32(half - max_exact)).astype(np.int32)
    large = np.minimum(large, half - 1)
    return ret + np.where(n < max_exact, n, large)


def _lookup_kernel(tab_ref, sel_ref, o_ref):
    sel = sel_ref[...]
    onehot = (lax.broadcasted_iota(jnp.int32, (tab_ref.shape[1], sel.shape[1]), 0) == sel).astype(F32)
    val = jnp.dot(tab_ref[...], onehot, preferred_element_type=F32, precision=lax.Precision.HIGHEST)
    o_ref[...] = jnp.where(sel >= 0, val, NEG)


def _lookup(table, sel):
    R, M = table.shape
    m_pad = -(-M // 8) * 8
    table = jnp.zeros((R, m_pad), F32).at[:, :M].set(table.astype(F32))
    n = sel.shape[0]
    tn = min(n, 4096)
    return pl.pallas_call(
        _lookup_kernel,
        grid=(n // tn,),
        in_specs=[pl.BlockSpec((R, m_pad), lambda j: (0, 0)), pl.BlockSpec((1, tn), lambda j: (0, j))],
        out_specs=pl.BlockSpec((R, tn), lambda j: (0, j)),
        out_shape=jax.ShapeDtypeStruct((R, n), F32),
        compiler_params=_cparams(("parallel",)),
        name="bias_lookup",
    )(table, jnp.asarray(sel.reshape(1, n), jnp.int32))


def _softmax_rows(s, extra=None):
    m = jnp.max(s, axis=-1, keepdims=True)
    if extra is not None:
        m = jnp.maximum(m, extra)
    e = jnp.exp(s - m)
    den = jnp.sum(e, axis=-1, keepdims=True)
    if extra is not None:
        den = den + jnp.exp(extra - m)
    return e / den


WIN_BLOCKS_PER_STEP = 4


def _win_attn_kernel(sink_ref, q_ref, kp_ref, kc_ref, kn_ref, vp_ref, vc_ref, vn_ref, bias_ref, o_ref,
                     s_scr, p_scr, *, seq):
    step = pl.program_id(1)
    k = jnp.concatenate([kp_ref[0], kc_ref[0], kn_ref[0]], axis=0)
    v = jnp.concatenate([vp_ref[0], vc_ref[0], vn_ref[0]], axis=0)
    span = WIN_BLOCK + 2 * WINDOW
    col = lax.broadcasted_iota(jnp.int32, (WIN_BLOCK, span), 1)
    low = lax.broadcasted_iota(jnp.int32, (WIN_BLOCK, LANES), 1) < HEAD_DIM
    zero = jnp.zeros((WIN_BLOCK, LANES), BF16)
    group = lambda a, blk, h: a[blk * WIN_BLOCK:blk * WIN_BLOCK + span,
                                (h // (N_HEADS_A // N_KV_A)) * LANES:(h // (N_HEADS_A // N_KV_A) + 1) * LANES]
    for blk in range(WIN_BLOCKS_PER_STEP):
        rows = slice(blk * WIN_BLOCK, (blk + 1) * WIN_BLOCK)
        kpos = col + (step * WIN_BLOCKS_PER_STEP + blk - 1) * WIN_BLOCK
        valid = (kpos >= 0) & (kpos < seq)
        for h in range(N_HEADS_A):
            qp = q_ref[0, rows, (h // 2) * LANES:(h // 2 + 1) * LANES]
            qh = jnp.where(low if h % 2 == 0 else jnp.logical_not(low), qp, zero)
            s_scr[h] = lax.dot_general(qh, group(k, blk, h), (((1,), (1,)), ((), ())),
                                       preferred_element_type=F32)
        for h in range(N_HEADS_A):
            s = jnp.where(valid, s_scr[h] + bias_ref[h], NEG)
            p_scr[h] = _softmax_rows(s, extra=sink_ref[h]).astype(BF16)
        for pair in range(N_HEADS_A // 2):
            outs = [jnp.dot(p_scr[2 * pair + par], group(v, blk, 2 * pair + par), preferred_element_type=F32)
                    for par in range(2)]
            o_ref[0, rows, pair * LANES:(pair + 1) * LANES] = jnp.where(low, outs[0], outs[1])


def _win_attn(qa, ka, va, sink, t5_table):
    B, S, _ = qa.shape
    nb = S // WIN_BLOCK
    per = WIN_BLOCKS_PER_STEP
    assert nb % per == 0
    span = WIN_BLOCK + 2 * WINDOW
    rel = np.arange(span)[None, :] - WINDOW - np.arange(WIN_BLOCK)[:, None]
    bucket = np.where(np.abs(rel) <= WINDOW, _t5_bucket_np(rel), -1)
    bias = _lookup(t5_table.T, bucket.reshape(-1)).reshape(N_HEADS_A, WIN_BLOCK, span)
    qspec = pl.BlockSpec((1, per * WIN_BLOCK, W_QA), lambda b, i: (b, i, 0))
    cur = pl.BlockSpec((1, per * WIN_BLOCK, 2 * W_KVA), lambda b, i: (b, i, 0))
    edge = lambda off: pl.BlockSpec((1, WIN_BLOCK, 2 * W_KVA),
                                    lambda b, i: (b, jnp.clip(i * per + off, 0, nb - 1), 0))
    return pl.pallas_call(
        functools.partial(_win_attn_kernel, seq=S),
        grid=(B, nb // per),
        in_specs=[pl.BlockSpec(memory_space=pltpu.SMEM), qspec,
                  edge(-1), cur, edge(per), edge(-1), cur, edge(per),
                  pl.BlockSpec(bias.shape, lambda b, i: (0, 0, 0))],
        out_specs=qspec,
        out_shape=jax.ShapeDtypeStruct((B, S, W_QA), F32),
        scratch_shapes=[pltpu.VMEM((N_HEADS_A, WIN_BLOCK, span), F32),
                        pltpu.VMEM((N_HEADS_A, WIN_BLOCK, span), BF16)],
        compiler_params=_cparams(("parallel", "parallel")),
        name="win_attn",
    )(sink.astype(F32), qa, ka, ka, ka, va, va, va, bias)


NBR_ROWS_PER_STEP = 8


def _nbr_attn_kernel(q_ref, kp_ref, kc_ref, kn_ref, vp_ref, vc_ref, vn_ref, bias_ref, o_ref,
                     k_scr, v_scr, s_scr, p_scr, *, n_rows):
    step = pl.program_id(1)
    blk = NBR_ROWS_PER_STEP * GRID_W
    for j, (kr, vr) in enumerate(((kp_ref, vp_ref), (kc_ref, vc_ref), (kn_ref, vn_ref))):
        k_scr[j * blk:(j + 1) * blk, :] = kr[0]
        v_scr[j * blk:(j + 1) * blk, :] = vr[0]
    low = lax.broadcasted_iota(jnp.int32, (GRID_W, LANES), 1) < HEAD_DIM
    zero = jnp.zeros((GRID_W, LANES), BF16)
    band = NA_ROWS * GRID_W

    def one_row(i, carry):
        r = step * NBR_ROWS_PER_STEP + i
        rs = jnp.clip(r - NA_ROWS // 2, 0, n_rows - NA_ROWS)
        off = pl.multiple_of((rs - (step - 1) * NBR_ROWS_PER_STEP) * GRID_W, GRID_W)
        d = r - rs
        q_rows = pl.ds(pl.multiple_of(i * GRID_W, GRID_W), GRID_W)
        pair_lanes = lambda h: slice((h // 2) * LANES, (h // 2 + 1) * LANES)
        for h in range(N_HEADS_B):
            qp = q_ref[0, q_rows, pair_lanes(h)]
            qh = jnp.where(low if h % 2 == 0 else jnp.logical_not(low), qp, zero)
            s_scr[h] = lax.dot_general(qh, k_scr[pl.ds(off, band), pair_lanes(h)], (((1,), (1,)), ((), ())),
                                       preferred_element_type=F32)
        for h in range(N_HEADS_B):
            p_scr[h] = _softmax_rows(s_scr[h] + bias_ref[d, h]).astype(BF16)
        for pair in range(N_HEADS_B // 2):
            sl = slice(pair * LANES, (pair + 1) * LANES)
            vp = v_scr[pl.ds(off, band), sl]
            outs = [jnp.dot(p_scr[2 * pair + par], vp, preferred_element_type=F32) for par in range(2)]
            o_ref[0, q_rows, sl] = jnp.where(low, outs[0], outs[1])
        return carry

    lax.fori_loop(0, NBR_ROWS_PER_STEP, one_row, 0)


def _nbr_bias(rpb):
    cols = np.arange(GRID_W)
    col_start = np.clip(cols - NA_COLS // 2, 0, GRID_W - NA_COLS)
    kc = np.arange(GRID_W)
    inside = (kc[None, :] >= col_start[:, None]) & (kc[None, :] < col_start[:, None] + NA_COLS)
    col_off = np.where(inside, kc[None, :] - cols[:, None] + (NA_COLS - 1), -1)
    table = jnp.stack([rpb[:, NA_ROWS - 1 - d:2 * NA_ROWS - 1 - d, :] for d in range(NA_ROWS)])
    b = _lookup(table.reshape(NA_ROWS * N_HEADS_B * NA_ROWS, 2 * NA_COLS - 1), col_off.reshape(-1))
    b = b.reshape(NA_ROWS, N_HEADS_B, NA_ROWS, GRID_W, GRID_W).transpose(0, 1, 3, 2, 4)
    return b.reshape(NA_ROWS, N_HEADS_B, GRID_W, NA_ROWS * GRID_W)


def _nbr_attn(qb, kb, vb, rpb):
    B, S, _ = qb.shape
    rows = S // GRID_W
    assert rows >= NA_ROWS
    assert rows % NBR_ROWS_PER_STEP == 0 and NBR_ROWS_PER_STEP >= NA_ROWS
    bias = _nbr_bias(rpb)
    n_steps = rows // NBR_ROWS_PER_STEP
    blk = NBR_ROWS_PER_STEP * GRID_W
    spec = lambda off: pl.BlockSpec((1, blk, W_B), lambda b, i: (b, jnp.clip(i + off, 0, n_steps - 1), 0))
    return pl.pallas_call(
        functools.partial(_nbr_attn_kernel, n_rows=rows),
        grid=(B, n_steps),
        in_specs=[spec(0), spec(-1), spec(0), spec(1), spec(-1), spec(0), spec(1),
                  pl.BlockSpec(bias.shape, lambda b, i: (0, 0, 0, 0), pipeline_mode=pl.Buffered(1))],
        out_specs=spec(0),
        out_shape=jax.ShapeDtypeStruct((B, S, W_B), F32),
        scratch_shapes=[pltpu.VMEM((3 * blk, W_B), BF16)] * 2
                       + [pltpu.VMEM((N_HEADS_B, GRID_W, NA_ROWS * GRID_W), F32),
                          pltpu.VMEM((N_HEADS_B, GRID_W, NA_ROWS * GRID_W), BF16)],
        compiler_params=_cparams(("parallel", "parallel")),
        name="nbr_attn",
    )(qb, kb, kb, kb, vb, vb, vb, bias)


def _out_proj_kernel(oa_ref, ob_ref, x_ref, mod_ref, ga_ref, gb_ref, wa_ref, wb_ref, g2_ref, x1_ref, h2_ref):
    na = _rms(oa_ref[0], ga_ref[...]).astype(BF16)
    nb = _rms(ob_ref[0], gb_ref[...]).astype(BF16)
    y = (jnp.dot(na, wa_ref[...], preferred_element_type=F32)
         + jnp.dot(nb, wb_ref[...], preferred_element_type=F32))
    x1 = x_ref[0] + mod_ref[0, 2:3, :] * y
    x1_ref[0] = x1
    h2_ref[0] = _rms(x1, g2_ref[...]) * (1.0 + mod_ref[0, 4:5, :]) + mod_ref[0, 3:4, :]


def _out_proj(o_a, o_b, x, mods, ga, gb, w_out, g2, tm):
    B, S, D = x.shape
    wb16 = w_out.astype(BF16)
    wa, wb = wb16[:W_QA], wb16[W_QA:]
    full = lambda a: pl.BlockSpec(a.shape, lambda b, i: (0, 0))
    tok = lambda w: pl.BlockSpec((1, tm, w), lambda b, i: (b, i, 0))
    return pl.pallas_call(
        _out_proj_kernel,
        grid=(B, S // tm),
        in_specs=[tok(W_QA), tok(W_B), tok(D), pl.BlockSpec((1, 6, D), lambda b, i: (b, 0, 0)),
                  full(ga), full(gb), full(wa), full(wb), full(g2)],
        out_specs=[tok(D), tok(D)],
        out_shape=[jax.ShapeDtypeStruct((B, S, D), F32)] * 2,
        compiler_params=_cparams(("parallel", "parallel")),
        name="out_proj",
    )(o_a, o_b, x, mods, ga, gb, wa, wb, g2)


def _topk_rows(s, k, payload=None):
    n_rows = s.shape[0]
    row = lax.broadcasted_iota(jnp.int32, s.shape, 0)
    vals, picked = [], []
    for _ in range(k):
        m = jnp.max(s, axis=0, keepdims=True)
        am = jnp.min(jnp.where(s == m, row, n_rows), axis=0, keepdims=True)
        hit = row == am
        vals.append(m)
        picked.append(am if payload is None else jnp.sum(jnp.where(hit, payload, 0), axis=0, keepdims=True))
        s = jnp.where(hit, -jnp.inf, s)
    return jnp.concatenate(vals, axis=0), jnp.concatenate(picked, axis=0)


def _product_candidates(v1, i1, v2, i2):
    assert PEER_TOPK == 16
    n = v1.shape[1]
    vals, ids = [], []
    for a in range(8):
        nb = PEER_TOPK if a == 0 else 8
        sv = v1[a:a + 1, :] + v2[:nb, :]
        limit = PEER_TOPK // (a + 1)
        if limit < nb:
            sv = jnp.where(lax.broadcasted_iota(jnp.int32, (nb, n), 0) < limit, sv, -jnp.inf)
        vals.append(sv)
        ids.append(i1[a:a + 1, :] * PEER_KEYS + i2[:nb, :])
    vals.append(v1[8:, :] + v2[0:1, :])
    ids.append(i1[8:, :] * PEER_KEYS + i2[0:1, :])
    return jnp.concatenate(vals, axis=0), jnp.concatenate(ids, axis=0)


def _route_kernel(h_ref, wq_ref, keys_ref, idx_ref, gate_ref):
    hb = h_ref[...].astype(BF16)
    idx_rows, gate_rows = [], []
    for h in range(PEER_HEADS):
        tops = []
        for side in range(2):
            c0 = h * PEER_QDIM + side * PEER_HALF
            qh = jnp.dot(hb, wq_ref[:, c0:c0 + PEER_HALF], preferred_element_type=F32).astype(BF16)
            s = lax.dot_general(keys_ref[h, side], qh, (((1,), (1,)), ((), ())),
                                preferred_element_type=F32)
            tops.append(_topk_rows(s, PEER_TOPK))
        (v1, i1), (v2, i2) = tops
        cand, cand_ids = _product_candidates(v1, i1, v2, i2)
        vs, ids = _topk_rows(cand, PEER_TOPK, payload=cand_ids)
        idx_rows.append(ids)
        ex = jnp.exp(vs - jnp.max(vs, axis=0, keepdims=True))
        gate_rows.append(ex / jnp.sum(ex, axis=0, keepdims=True))
    idx_ref[...] = (jnp.concatenate(idx_rows, axis=0) * SLAB_ROWS).T
    gate_ref[...] = jnp.concatenate(gate_rows, axis=0).T


def _route(h2, w_query, sub_keys, tm):
    T, D = h2.shape
    wq = w_query.astype(BF16)
    keys = sub_keys.astype(BF16)
    return pl.pallas_call(
        _route_kernel,
        grid=(T // tm,),
        in_specs=[pl.BlockSpec((tm, D), lambda i: (i, 0)),
                  pl.BlockSpec(wq.shape, lambda i: (0, 0)),
                  pl.BlockSpec(keys.shape, lambda i: (0, 0, 0, 0))],
        out_specs=[pl.BlockSpec((tm, N_PICKS), lambda i: (i, 0))] * 2,
        out_shape=[jax.ShapeDtypeStruct((T, N_PICKS), jnp.int32),
                   jax.ShapeDtypeStruct((T, N_PICKS), F32)],
        compiler_params=_cparams(("parallel",)),
        name="route",
    )(h2, wq, keys)


SLAB_ROWS = 4
FEATURE_ROWS = 8
TOKENS_PER_ITER = 16


def _pack_rows_kernel(x_ref, o_ref):
    o_ref[...] = pltpu.bitcast(x_ref[...], jnp.int32)


def _packed_expert_table(experts):
    n_experts, D = experts.shape
    assert D == FEATURE_ROWS * LANES
    rows = experts.astype(BF16).reshape(n_experts * FEATURE_ROWS, LANES)
    tile = 2048 * FEATURE_ROWS
    return pl.pallas_call(
        _pack_rows_kernel,
        grid=(rows.shape[0] // tile,),
        in_specs=[pl.BlockSpec((tile, LANES), lambda i: (i, 0))],
        out_specs=pl.BlockSpec((tile // 2, LANES), lambda i: (i, 0)),
        out_shape=jax.ShapeDtypeStruct((rows.shape[0] // 2, LANES), jnp.int32),
        compiler_params=_cparams(("parallel",)),
        name="pack_rows",
    )(rows)


def _token_expert_rows(tbl_ref, picks):
    pairs = []
    for m in range(N_PICKS // 2):
        words = jnp.concatenate(
            [tbl_ref[pl.ds(pl.multiple_of(picks[2 * m + j], SLAB_ROWS), SLAB_ROWS), :] for j in range(2)], axis=0)
        pairs.append(pltpu.bitcast(words, BF16))
    return jnp.concatenate(pairs, axis=0)


def _own_row_mask():
    shape = (FEATURE_ROWS, N_PICKS * FEATURE_ROWS)
    return lax.broadcasted_iota(jnp.int32, shape, 1) % FEATURE_ROWS == lax.broadcasted_iota(jnp.int32, shape, 0)


def _token_loop(n_tokens, one_token):
    def body(i, carry):
        for u in range(TOKENS_PER_ITER):
            one_token(i * TOKENS_PER_ITER + u)
        return carry

    lax.fori_loop(0, n_tokens // TOKENS_PER_ITER, body, 0)


def _peer_u_kernel(rows_ref, x_ref, tbl_ref, z_ref, y_scr):
    own = _own_row_mask()

    def token(t):
        rows = _token_expert_rows(tbl_ref, rows_ref.at[t])
        x_row = x_ref[pl.ds(t, 1), :].astype(BF16)
        x_rows = jnp.concatenate([x_row[:, r * LANES:(r + 1) * LANES] for r in range(FEATURE_ROWS)], axis=0)
        part = lax.dot_general(x_rows, rows, (((1,), (1,)), ((), ())),
                               preferred_element_type=F32)
        y_scr[pl.ds(t, 1), :] = jnp.sum(jnp.where(own, part, 0.0), axis=0, keepdims=True)

    _token_loop(x_ref.shape[0], token)
    shape = (N_PICKS * FEATURE_ROWS, N_PICKS)
    group = (lax.broadcasted_iota(jnp.int32, shape, 0) // FEATURE_ROWS
             == lax.broadcasted_iota(jnp.int32, shape, 1)).astype(F32)
    z_ref[...] = jnp.dot(y_scr[...], group, preferred_element_type=F32, precision=lax.Precision.HIGHEST)


def _gelu_tanh(z):
    return 0.5 * z * (1.0 + jnp.tanh(math.sqrt(2.0 / math.pi) * (z + 0.044715 * (z * z * z))))


def _peer_v_kernel(rows_ref, z_ref, gate_ref, tbl_ref, o_ref, w_scr):
    a = (_gelu_tanh(z_ref[...]) * gate_ref[...]).astype(BF16)
    shape = (N_PICKS, N_PICKS * FEATURE_ROWS)
    spread = (lax.broadcasted_iota(jnp.int32, shape, 1) // FEATURE_ROWS
              == lax.broadcasted_iota(jnp.int32, shape, 0)).astype(BF16)
    w_scr[...] = jnp.dot(a, spread, preferred_element_type=F32)
    own = _own_row_mask()

    def token(t):
        rows = _token_expert_rows(tbl_ref, rows_ref.at[t])
        left = jnp.where(own, w_scr[pl.ds(t, 1), :], 0.0).astype(BF16)
        o_ref[t] = jnp.dot(left, rows, preferred_element_type=F32)

    _token_loop(o_ref.shape[0], token)


def _peer(h2, rows, gate, u_experts, v_experts, tb):
    T, D = h2.shape
    assert tb % TOKENS_PER_ITER == 0
    rows_spec = pl.BlockSpec((tb, N_PICKS), lambda i: (i, 0), memory_space=pltpu.SMEM)
    picks_spec = pl.BlockSpec((tb, N_PICKS), lambda i: (i, 0))
    feat_spec = pl.BlockSpec((tb, FEATURE_ROWS, LANES), lambda i: (i, 0, 0))
    table_spec = pl.BlockSpec((u_experts.shape[0] * SLAB_ROWS, LANES), lambda i: (0, 0),
                              pipeline_mode=pl.Buffered(1))
    wide = N_PICKS * FEATURE_ROWS
    z = pl.pallas_call(
        _peer_u_kernel,
        grid=(T // tb,),
        in_specs=[rows_spec, pl.BlockSpec((tb, D), lambda i: (i, 0)), table_spec],
        out_specs=picks_spec,
        out_shape=jax.ShapeDtypeStruct((T, N_PICKS), F32),
        scratch_shapes=[pltpu.VMEM((tb, wide), F32)],
        compiler_params=_cparams(("arbitrary",)),
        name="peer_u",
    )(rows, h2, _packed_expert_table(u_experts))
    out = pl.pallas_call(
        _peer_v_kernel,
        grid=(T // tb,),
        in_specs=[rows_spec, picks_spec, picks_spec, table_spec],
        out_specs=feat_spec,
        out_shape=jax.ShapeDtypeStruct((T, FEATURE_ROWS, LANES), F32),
        scratch_shapes=[pltpu.VMEM((tb, wide), F32)],
        compiler_params=_cparams(("arbitrary",)),
        name="peer_v",
    )(rows, z, gate, _packed_expert_table(v_experts))
    return out.reshape(T, D)


def _final_kernel(x1_ref, p_ref, mod_ref, g_ref, o_ref):
    o_ref[0] = _rms(x1_ref[0] + mod_ref[0, 5:6, :] * p_ref[0], g_ref[...])


def _final(x1, peer_out, mods, g, tm):
    B, S, D = x1.shape
    tok = pl.BlockSpec((1, tm, D), lambda b, i: (b, i, 0))
    return pl.pallas_call(
        _final_kernel,
        grid=(B, S // tm),
        in_specs=[tok, tok, pl.BlockSpec((1, 6, D), lambda b, i: (b, 0, 0)),
                  pl.BlockSpec((1, D), lambda b, i: (0, 0))],
        out_specs=tok,
        out_shape=jax.ShapeDtypeStruct((B, S, D), F32),
        compiler_params=_cparams(("parallel", "parallel")),
        name="final",
    )(x1, peer_out, mods, g)


def _token_tile(S):
    return min(512, S)


def kernel(x, c, w_ada, b_ada, norm1_g, w_in, sink_a, t5_table, rpb_b, out_norm_a, out_norm_b,
           w_out, norm2_g, w_query, sub_keys, u_experts, v_experts, final_g):
    B, S, D = x.shape
    assert w_ada.shape[0] == 1, "the last residual add is fused with the final norm: one layer only"
    tm = _token_tile(S)
    mods = _ada(c, w_ada[0], b_ada[0])
    qa, ka, va, qb, kb, vb = _in_proj(x, mods, norm1_g[0].reshape(1, D), w_in[0], tm)
    o_a = _win_attn(qa, ka, va, sink_a[0], t5_table)
    o_b = _nbr_attn(qb, kb, vb, rpb_b[0])
    x1, h2 = _out_proj(o_a, o_b, x, mods, out_norm_a[0].reshape(1, W_QA), out_norm_b[0].reshape(1, W_B),
                       w_out[0], norm2_g[0].reshape(1, D), tm)
    h2f = h2.reshape(B * S, D)
    rows, gate = _route(h2f, w_query[0], sub_keys[0], min(256, B * S))
    peer_out = _peer(h2f, rows, gate, u_experts[0], v_experts[0], min(128, B * S))
    return _final(x1, peer_out.reshape(B, S, D), mods, final_g.reshape(1, D), tm)
```

```python
import functools
import math

import numpy as np
import jax
import jax.numpy as jnp
from jax import lax
from jax.experimental import pallas as pl
from jax.experimental.pallas import tpu as pltpu

F32 = jnp.float32
BF16 = jnp.bfloat16

HEAD_DIM = 64
N_HEADS_A = 8
N_KV_A = 2
N_HEADS_B = 8
WINDOW = 128
WIN_BLOCK = 128
T5_BUCKETS = 32
T5_MAX_DIST = 128
GRID_W = 64
NA_ROWS = 8
NA_COLS = 16
PEER_HEADS = 8
PEER_KEYS = 128
PEER_QDIM = 256
PEER_HALF = 128
PEER_TOPK = 16
N_PICKS = PEER_HEADS * PEER_TOPK
EPS = 1e-6
NEG = -1e30

W_QA = N_HEADS_A * HEAD_DIM
W_KVA = N_KV_A * HEAD_DIM
W_B = N_HEADS_B * HEAD_DIM

LANES = 128
VMEM_LIMIT = 60 * 1024 * 1024


def _cparams(sem):
    return pltpu.CompilerParams(dimension_semantics=sem, vmem_limit_bytes=VMEM_LIMIT)


def _ada_kernel(c_ref, w_ref, b_ref, o_ref):
    c = c_ref[...]
    act = c * jax.nn.sigmoid(c)
    o_ref[...] = jnp.dot(act, w_ref[...], preferred_element_type=F32,
                         precision=lax.Precision.HIGHEST) + b_ref[...]


def _ada(c, w_ada, b_ada):
    B, D = c.shape
    n_out = w_ada.shape[1]
    rows = 8
    c_pad = jnp.zeros((rows, D), F32).at[:B].set(c)
    out = pl.pallas_call(
        _ada_kernel,
        grid=(n_out // D,),
        in_specs=[pl.BlockSpec((rows, D), lambda j: (0, 0)),
                  pl.BlockSpec((D, D), lambda j: (0, j)),
                  pl.BlockSpec((1, D), lambda j: (0, j))],
        out_specs=pl.BlockSpec((rows, D), lambda j: (0, j)),
        out_shape=jax.ShapeDtypeStruct((rows, n_out), F32),
        compiler_params=_cparams(("arbitrary",)),
        name="ada",
    )(c_pad, w_ada, b_ada.reshape(1, n_out))
    return out[:B].reshape(B, n_out // D, D)


def _rms(x, g):
    return x * lax.rsqrt(jnp.mean(x * x, axis=-1, keepdims=True) + EPS) * g


def _in_proj_kernel(x_ref, mod_ref, g_ref, wq_ref, wk_ref, wv_ref, wqb_ref, wkb_ref, wvb_ref,
                    qa_ref, ka_ref, va_ref, qb_ref, kb_ref, vb_ref):
    x = x_ref[0]
    h = _rms(x, g_ref[...]) * (1.0 + mod_ref[0, 1:2, :]) + mod_ref[0, 0:1, :]
    hb = h.astype(BF16)
    scale = HEAD_DIM ** -0.5
    qa_ref[0] = (jnp.dot(hb, wq_ref[...], preferred_element_type=F32) * scale).astype(BF16)
    ka_ref[0] = jnp.dot(hb, wk_ref[...], preferred_element_type=F32).astype(BF16)
    va_ref[0] = jnp.dot(hb, wv_ref[...], preferred_element_type=F32).astype(BF16)
    qb_ref[0] = (jnp.dot(hb, wqb_ref[...], preferred_element_type=F32) * scale).astype(BF16)
    kb_ref[0] = jnp.dot(hb, wkb_ref[...], preferred_element_type=F32).astype(BF16)
    vb_ref[0] = jnp.dot(hb, wvb_ref[...], preferred_element_type=F32).astype(BF16)


def _dup_kv_cols(w):
    a, b = w[:, :HEAD_DIM], w[:, HEAD_DIM:]
    return jnp.concatenate([a, a, b, b], axis=1)


def _in_proj(x, mods, g1, w_in, tm):
    B, S, D = x.shape
    wb = w_in.astype(BF16)
    o = 0
    wq = wb[:, o:o + W_QA]; o += W_QA
    wk = _dup_kv_cols(wb[:, o:o + W_KVA]); o += W_KVA
    wv = _dup_kv_cols(wb[:, o:o + W_KVA]); o += W_KVA
    wqb = wb[:, o:o + W_B]; o += W_B
    wkb = wb[:, o:o + W_B]; o += W_B
    wvb = wb[:, o:o + W_B]
    full = lambda a: pl.BlockSpec(a.shape, lambda b, i: (0, 0))
    tok = lambda w: pl.BlockSpec((1, tm, w), lambda b, i: (b, i, 0))
    widths = (W_QA, 2 * W_KVA, 2 * W_KVA, W_B, W_B, W_B)
    return pl.pallas_call(
        _in_proj_kernel,
        grid=(B, S // tm),
        in_specs=[tok(D), pl.BlockSpec((1, 6, D), lambda b, i: (b, 0, 0)), full(g1),
                  full(wq), full(wk), full(wv), full(wqb), full(wkb), full(wvb)],
        out_specs=[tok(w) for w in widths],
        out_shape=[jax.ShapeDtypeStruct((B, S, w), BF16) for w in widths],
        compiler_params=_cparams(("parallel", "parallel")),
        name="in_proj",
    )(x, mods, g1, wq, wk, wv, wqb, wkb, wvb)


def _t5_bucket_np(rel):
    half = T5_BUCKETS // 2
    max_exact = half // 2
    ret = np.where(rel > 0, half, 0)
    n = np.abs(rel)
    nf = np.maximum(n, 1).astype(np.float32)
    large = max_exact + (np.log(nf / np.float32(max_exact)) / np.float32(math.log(T5_MAX_DIST / max_exact))
                         * np.float32(half - max_exact)).astype(np.int32)
    large = np.minimum(large, half - 1)
    return ret + np.where(n < max_exact, n, large)


def _lookup_kernel(tab_ref, sel_ref, o_ref):
    sel = sel_ref[...]
    onehot = (lax.broadcasted_iota(jnp.int32, (tab_ref.shape[1], sel.shape[1]), 0) == sel).astype(F32)
    val = jnp.dot(tab_ref[...], onehot, preferred_element_type=F32, precision=lax.Precision.HIGHEST)
    o_ref[...] = jnp.where(sel >= 0, val, NEG)


def _lookup(table, sel):
    R, M = table.shape
    m_pad = -(-M // 8) * 8
    table = jnp.zeros((R, m_pad), F32).at[:, :M].set(table.astype(F32))
    n = sel.shape[0]
    tn = min(n, 4096)
    return pl.pallas_call(
        _lookup_kernel,
        grid=(n // tn,),
        in_specs=[pl.BlockSpec((R, m_pad), lambda j: (0, 0)), pl.BlockSpec((1, tn), lambda j: (0, j))],
        out_specs=pl.BlockSpec((R, tn), lambda j: (0, j)),
        out_shape=jax.ShapeDtypeStruct((R, n), F32),
        compiler_params=_cparams(("parallel",)),
        name="bias_lookup",
    )(table, jnp.asarray(sel.reshape(1, n), jnp.int32))


def _softmax_rows(s, extra=None):
    m = jnp.max(s, axis=-1, keepdims=True)
    if extra is not None:
        m = jnp.maximum(m, extra)
    e = jnp.exp(s - m)
    den = jnp.sum(e, axis=-1, keepdims=True)
    if extra is not None:
        den = den + jnp.exp(extra - m)
    return e / den


WIN_BLOCKS_PER_STEP = 4


def _win_attn_kernel(sink_ref, q_ref, kp_ref, kc_ref, kn_ref, vp_ref, vc_ref, vn_ref, bias_ref, o_ref,
                     s_scr, p_scr, *, seq):
    step = pl.program_id(1)
    k = jnp.concatenate([kp_ref[0], kc_ref[0], kn_ref[0]], axis=0)
    v = jnp.concatenate([vp_ref[0], vc_ref[0], vn_ref[0]], axis=0)
    span = WIN_BLOCK + 2 * WINDOW
    col = lax.broadcasted_iota(jnp.int32, (WIN_BLOCK, span), 1)
    low = lax.broadcasted_iota(jnp.int32, (WIN_BLOCK, LANES), 1) < HEAD_DIM
    zero = jnp.zeros((WIN_BLOCK, LANES), BF16)
    group = lambda a, blk, h: a[blk * WIN_BLOCK:blk * WIN_BLOCK + span,
                                (h // (N_HEADS_A // N_KV_A)) * LANES:(h // (N_HEADS_A // N_KV_A) + 1) * LANES]
    for blk in range(WIN_BLOCKS_PER_STEP):
        rows = slice(blk * WIN_BLOCK, (blk + 1) * WIN_BLOCK)
        kpos = col + (step * WIN_BLOCKS_PER_STEP + blk - 1) * WIN_BLOCK
        valid = (kpos >= 0) & (kpos < seq)
        for h in range(N_HEADS_A):
            qp = q_ref[0, rows, (h // 2) * LANES:(h // 2 + 1) * LANES]
            qh = jnp.where(low if h % 2 == 0 else jnp.logical_not(low), qp, zero)
            s_scr[h] = lax.dot_general(qh, group(k, blk, h), (((1,), (1,)), ((), ())),
                                       preferred_element_type=F32)
        for h in range(N_HEADS_A):
            s = jnp.where(valid, s_scr[h] + bias_ref[h], NEG)
            p_scr[h] = _softmax_rows(s, extra=sink_ref[h]).astype(BF16)
        for pair in range(N_HEADS_A // 2):
            outs = [jnp.dot(p_scr[2 * pair + par], group(v, blk, 2 * pair + par), preferred_element_type=F32)
                    for par in range(2)]
            o_ref[0, rows, pair * LANES:(pair + 1) * LANES] = jnp.where(low, outs[0], outs[1])


def _win_attn(qa, ka, va, sink, t5_table):
    B, S, _ = qa.shape
    nb = S // WIN_BLOCK
    per = WIN_BLOCKS_PER_STEP
    assert nb % per == 0
    span = WIN_BLOCK + 2 * WINDOW
    rel = np.arange(span)[None, :] - WINDOW - np.arange(WIN_BLOCK)[:, None]
    bucket = np.where(np.abs(rel) <= WINDOW, _t5_bucket_np(rel), -1)
    bias = _lookup(t5_table.T, bucket.reshape(-1)).reshape(N_HEADS_A, WIN_BLOCK, span)
    qspec = pl.BlockSpec((1, per * WIN_BLOCK, W_QA), lambda b, i: (b, i, 0))
    cur = pl.BlockSpec((1, per * WIN_BLOCK, 2 * W_KVA), lambda b, i: (b, i, 0))
    edge = lambda off: pl.BlockSpec((1, WIN_BLOCK, 2 * W_KVA),
                                    lambda b, i: (b, jnp.clip(i * per + off, 0, nb - 1), 0))
    return pl.pallas_call(
        functools.partial(_win_attn_kernel, seq=S),
        grid=(B, nb // per),
        in_specs=[pl.BlockSpec(memory_space=pltpu.SMEM), qspec,
                  edge(-1), cur, edge(per), edge(-1), cur, edge(per),
                  pl.BlockSpec(bias.shape, lambda b, i: (0, 0, 0))],
        out_specs=qspec,
        out_shape=jax.ShapeDtypeStruct((B, S, W_QA), F32),
        scratch_shapes=[pltpu.VMEM((N_HEADS_A, WIN_BLOCK, span), F32),
                        pltpu.VMEM((N_HEADS_A, WIN_BLOCK, span), BF16)],
        compiler_params=_cparams(("parallel", "parallel")),
        name="win_attn",
    )(sink.astype(F32), qa, ka, ka, ka, va, va, va, bias)


NBR_ROWS_PER_STEP = 8


def _nbr_attn_kernel(q_ref, kp_ref, kc_ref, kn_ref, vp_ref, vc_ref, vn_ref, bias_ref, o_ref,
                     k_scr, v_scr, s_scr, p_scr, *, n_rows):
    step = pl.program_id(1)
    blk = NBR_ROWS_PER_STEP * GRID_W
    for j, (kr, vr) in enumerate(((kp_ref, vp_ref), (kc_ref, vc_ref), (kn_ref, vn_ref))):
        k_scr[j * blk:(j + 1) * blk, :] = kr[0]
        v_scr[j * blk:(j + 1) * blk, :] = vr[0]
    low = lax.broadcasted_iota(jnp.int32, (GRID_W, LANES), 1) < HEAD_DIM
    zero = jnp.zeros((GRID_W, LANES), BF16)
    band = NA_ROWS * GRID_W

    def one_row(i, carry):
        r = step * NBR_ROWS_PER_STEP + i
        rs = jnp.clip(r - NA_ROWS // 2, 0, n_rows - NA_ROWS)
        off = pl.multiple_of((rs - (step - 1) * NBR_ROWS_PER_STEP) * GRID_W, GRID_W)
        d = r - rs
        q_rows = pl.ds(pl.multiple_of(i * GRID_W, GRID_W), GRID_W)
        pair_lanes = lambda h: slice((h // 2) * LANES, (h // 2 + 1) * LANES)
        for h in range(N_HEADS_B):
            qp = q_ref[0, q_rows, pair_lanes(h)]
            qh = jnp.where(low if h % 2 == 0 else jnp.logical_not(low), qp, zero)
            s_scr[h] = lax.dot_general(qh, k_scr[pl.ds(off, band), pair_lanes(h)], (((1,), (1,)), ((), ())),
                                       preferred_element_type=F32)
        for h in range(N_HEADS_B):
            p_scr[h] = _softmax_rows(s_scr[h] + bias_ref[d, h]).astype(BF16)
        for pair in range(N_HEADS_B // 2):
            sl = slice(pair * LANES, (pair + 1) * LANES)
            vp = v_scr[pl.ds(off, band), sl]
            outs = [jnp.dot(p_scr[2 * pair + par], vp, preferred_element_type=F32) for par in range(2)]
            o_ref[0, q_rows, sl] = jnp.where(low, outs[0], outs[1])
        return carry

    lax.fori_loop(0, NBR_ROWS_PER_STEP, one_row, 0)


def _nbr_bias(rpb):
    cols = np.arange(GRID_W)
    col_start = np.clip(cols - NA_COLS // 2, 0, GRID_W - NA_COLS)
    kc = np.arange(GRID_W)
    inside = (kc[None, :] >= col_start[:, None]) & (kc[None, :] < col_start[:, None] + NA_COLS)
    col_off = np.where(inside, kc[None, :] - cols[:, None] + (NA_COLS - 1), -1)
    table = jnp.stack([rpb[:, NA_ROWS - 1 - d:2 * NA_ROWS - 1 - d, :] for d in range(NA_ROWS)])
    b = _lookup(table.reshape(NA_ROWS * N_HEADS_B * NA_ROWS, 2 * NA_COLS - 1), col_off.reshape(-1))
    b = b.reshape(NA_ROWS, N_HEADS_B, NA_ROWS, GRID_W, GRID_W).transpose(0, 1, 3, 2, 4)
    return b.reshape(NA_ROWS, N_HEADS_B, GRID_W, NA_ROWS * GRID_W)


def _nbr_attn(qb, kb, vb, rpb):
    B, S, _ = qb.shape
    rows = S // GRID_W
    assert rows >= NA_ROWS
    assert rows % NBR_ROWS_PER_STEP == 0 and NBR_ROWS_PER_STEP >= NA_ROWS
    bias = _nbr_bias(rpb)
    n_steps = rows // NBR_ROWS_PER_STEP
    blk = NBR_ROWS_PER_STEP * GRID_W
    spec = lambda off: pl.BlockSpec((1, blk, W_B), lambda b, i: (b, jnp.clip(i + off, 0, n_steps - 1), 0))
    return pl.pallas_call(
        functools.partial(_nbr_attn_kernel, n_rows=rows),
        grid=(B, n_steps),
        in_specs=[spec(0), spec(-1), spec(0), spec(1), spec(-1), spec(0), spec(1),
                  pl.BlockSpec(bias.shape, lambda b, i: (0, 0, 0, 0), pipeline_mode=pl.Buffered(1))],
        out_specs=spec(0),
        out_shape=jax.ShapeDtypeStruct((B, S, W_B), F32),
        scratch_shapes=[pltpu.VMEM((3 * blk, W_B), BF16)] * 2
                       + [pltpu.VMEM((N_HEADS_B, GRID_W, NA_ROWS * GRID_W), F32),
                          pltpu.VMEM((N_HEADS_B, GRID_W, NA_ROWS * GRID_W), BF16)],
        compiler_params=_cparams(("parallel", "parallel")),
        name="nbr_attn",
    )(qb, kb, kb, kb, vb, vb, vb, bias)


def _out_proj_kernel(oa_ref, ob_ref, x_ref, mod_ref, ga_ref, gb_ref, wa_ref, wb_ref, g2_ref, x1_ref, h2_ref):
    na = _rms(oa_ref[0], ga_ref[...]).astype(BF16)
    nb = _rms(ob_ref[0], gb_ref[...]).astype(BF16)
    y = (jnp.dot(na, wa_ref[...], preferred_element_type=F32)
         + jnp.dot(nb, wb_ref[...], preferred_element_type=F32))
    x1 = x_ref[0] + mod_ref[0, 2:3, :] * y
    x1_ref[0] = x1
    h2_ref[0] = _rms(x1, g2_ref[...]) * (1.0 + mod_ref[0, 4:5, :]) + mod_ref[0, 3:4, :]


def _out_proj(o_a, o_b, x, mods, ga, gb, w_out, g2, tm):
    B, S, D = x.shape
    wb16 = w_out.astype(BF16)
    wa, wb = wb16[:W_QA], wb16[W_QA:]
    full = lambda a: pl.BlockSpec(a.shape, lambda b, i: (0, 0))
    tok = lambda w: pl.BlockSpec((1, tm, w), lambda b, i: (b, i, 0))
    return pl.pallas_call(
        _out_proj_kernel,
        grid=(B, S // tm),
        in_specs=[tok(W_QA), tok(W_B), tok(D), pl.BlockSpec((1, 6, D), lambda b, i: (b, 0, 0)),
                  full(ga), full(gb), full(wa), full(wb), full(g2)],
        out_specs=[tok(D), tok(D)],
        out_shape=[jax.ShapeDtypeStruct((B, S, D), F32)] * 2,
        compiler_params=_cparams(("parallel", "parallel")),
        name="out_proj",
    )(o_a, o_b, x, mods, ga, gb, wa, wb, g2)


def _topk_rows(s, k, payload=None):
    n_rows = s.shape[0]
    row = lax.broadcasted_iota(jnp.int32, s.shape, 0)
    vals, picked = [], []
    for _ in range(k):
        m = jnp.max(s, axis=0, keepdims=True)
        am = jnp.min(jnp.where(s == m, row, n_rows), axis=0, keepdims=True)
        hit = row == am
        vals.append(m)
        picked.append(am if payload is None else jnp.sum(jnp.where(hit, payload, 0), axis=0, keepdims=True))
        s = jnp.where(hit, -jnp.inf, s)
    return jnp.concatenate(vals, axis=0), jnp.concatenate(picked, axis=0)


def _product_candidates(v1, i1, v2, i2):
    assert PEER_TOPK == 16
    n = v1.shape[1]
    vals, ids = [], []
    for a in range(8):
        nb = PEER_TOPK if a == 0 else 8
        sv = v1[a:a + 1, :] + v2[:nb, :]
        limit = PEER_TOPK // (a + 1)
        if limit < nb:
            sv = jnp.where(lax.broadcasted_iota(jnp.int32, (nb, n), 0) < limit, sv, -jnp.inf)
        vals.append(sv)
        ids.append(i1[a:a + 1, :] * PEER_KEYS + i2[:nb, :])
    vals.append(v1[8:, :] + v2[0:1, :])
    ids.append(i1[8:, :] * PEER_KEYS + i2[0:1, :])
    return jnp.concatenate(vals, axis=0), jnp.concatenate(ids, axis=0)


def _route_kernel(h_ref, wq_ref, keys_ref, idx_ref, gate_ref):
    hb = h_ref[...].astype(BF16)
    idx_rows, gate_rows = [], []
    for h in range(PEER_HEADS):
        tops = []
        for side in range(2):
            c0 = h * PEER_QDIM + side * PEER_HALF
            qh = jnp.dot(hb, wq_ref[:, c0:c0 + PEER_HALF], preferred_element_type=F32).astype(BF16)
            s = lax.dot_general(keys_ref[h, side], qh, (((1,), (1,)), ((), ())),
                                preferred_element_type=F32)
            tops.append(_topk_rows(s, PEER_TOPK))
        (v1, i1), (v2, i2) = tops
        cand, cand_ids = _product_candidates(v1, i1, v2, i2)
        vs, ids = _topk_rows(cand, PEER_TOPK, payload=cand_ids)
        idx_rows.append(ids)
        ex = jnp.exp(vs - jnp.max(vs, axis=0, keepdims=True))
        gate_rows.append(ex / jnp.sum(ex, axis=0, keepdims=True))
    idx_ref[...] = (jnp.concatenate(idx_rows, axis=0) * SLAB_ROWS).T
    gate_ref[...] = jnp.concatenate(gate_rows, axis=0).T


def _route(h2, w_query, sub_keys, tm):
    T, D = h2.shape
    wq = w_query.astype(BF16)
    keys = sub_keys.astype(BF16)
    return pl.pallas_call(
        _route_kernel,
        grid=(T // tm,),
        in_specs=[pl.BlockSpec((tm, D), lambda i: (i, 0)),
                  pl.BlockSpec(wq.shape, lambda i: (0, 0)),
                  pl.BlockSpec(keys.shape, lambda i: (0, 0, 0, 0))],
        out_specs=[pl.BlockSpec((tm, N_PICKS), lambda i: (i, 0))] * 2,
        out_shape=[jax.ShapeDtypeStruct((T, N_PICKS), jnp.int32),
                   jax.ShapeDtypeStruct((T, N_PICKS), F32)],
        compiler_params=_cparams(("parallel",)),
        name="route",
    )(h2, wq, keys)


SLAB_ROWS = 4
FEATURE_ROWS = 8
TOKENS_PER_ITER = 16


def _pack_rows_kernel(x_ref, o_ref):
    o_ref[...] = pltpu.bitcast(x_ref[...], jnp.int32)


def _packed_expert_table(experts):
    n_experts, D = experts.shape
    assert D == FEATURE_ROWS * LANES
    rows = experts.astype(BF16).reshape(n_experts * FEATURE_ROWS, LANES)
    tile = 2048 * FEATURE_ROWS
    return pl.pallas_call(
        _pack_rows_kernel,
        grid=(rows.shape[0] // tile,),
        in_specs=[pl.BlockSpec((tile, LANES), lambda i: (i, 0))],
        out_specs=pl.BlockSpec((tile // 2, LANES), lambda i: (i, 0)),
        out_shape=jax.ShapeDtypeStruct((rows.shape[0] // 2, LANES), jnp.int32),
        compiler_params=_cparams(("parallel",)),
        name="pack_rows",
    )(rows)


def _token_expert_rows(tbl_ref, picks):
    pairs = []
    for m in range(N_PICKS // 2):
        words = jnp.concatenate(
            [tbl_ref[pl.ds(pl.multiple_of(picks[2 * m + j], SLAB_ROWS), SLAB_ROWS), :] for j in range(2)], axis=0)
        pairs.append(pltpu.bitcast(words, BF16))
    return jnp.concatenate(pairs, axis=0)


def _own_row_mask():
    shape = (FEATURE_ROWS, N_PICKS * FEATURE_ROWS)
    return lax.broadcasted_iota(jnp.int32, shape, 1) % FEATURE_ROWS == lax.broadcasted_iota(jnp.int32, shape, 0)


def _token_loop(n_tokens, one_token):
    def body(i, carry):
        for u in range(TOKENS_PER_ITER):
            one_token(i * TOKENS_PER_ITER + u)
        return carry

    lax.fori_loop(0, n_tokens // TOKENS_PER_ITER, body, 0)


def _peer_u_kernel(rows_ref, x_ref, tbl_ref, z_ref, y_scr):
    own = _own_row_mask()

    def token(t):
        rows = _token_expert_rows(tbl_ref, rows_ref.at[t])
        x_row = x_ref[pl.ds(t, 1), :].astype(BF16)
        x_rows = jnp.concatenate([x_row[:, r * LANES:(r + 1) * LANES] for r in range(FEATURE_ROWS)], axis=0)
        part = lax.dot_general(x_rows, rows, (((1,), (1,)), ((), ())),
                               preferred_element_type=F32)
        y_scr[pl.ds(t, 1), :] = jnp.sum(jnp.where(own, part, 0.0), axis=0, keepdims=True)

    _token_loop(x_ref.shape[0], token)
    shape = (N_PICKS * FEATURE_ROWS, N_PICKS)
    group = (lax.broadcasted_iota(jnp.int32, shape, 0) // FEATURE_ROWS
             == lax.broadcasted_iota(jnp.int32, shape, 1)).astype(F32)
    z_ref[...] = jnp.dot(y_scr[...], group, preferred_element_type=F32, precision=lax.Precision.HIGHEST)


def _gelu_tanh(z):
    return 0.5 * z * (1.0 + jnp.tanh(math.sqrt(2.0 / math.pi) * (z + 0.044715 * (z * z * z))))


def _peer_v_kernel(rows_ref, z_ref, gate_ref, tbl_ref, x1_ref, mod_ref, g_ref, o_ref, w_scr):
    a = (_gelu_tanh(z_ref[...]) * gate_ref[...]).astype(BF16)
    shape = (N_PICKS, N_PICKS * FEATURE_ROWS)
    spread = (lax.broadcasted_iota(jnp.int32, shape, 1) // FEATURE_ROWS
              == lax.broadcasted_iota(jnp.int32, shape, 0)).astype(BF16)
    w_scr[...] = jnp.dot(a, spread, preferred_element_type=F32)
    own = _own_row_mask()

    def token(t):
        rows = _token_expert_rows(tbl_ref, rows_ref.at[t])
        left = jnp.where(own, w_scr[pl.ds(t, 1), :], 0.0).astype(BF16)
        feat = jnp.dot(left, rows, preferred_element_type=F32)
        o_ref[pl.ds(t, 1), :] = jnp.concatenate([feat[r:r + 1, :] for r in range(FEATURE_ROWS)], axis=1)

    _token_loop(o_ref.shape[0], token)
    o_ref[...] = _rms(x1_ref[...] + mod_ref[0, 5:6, :] * o_ref[...], g_ref[...])


def _peer_and_final(h2, rows, gate, u_experts, v_experts, x1, mods, final_g, seq, tb):
    T, D = h2.shape
    assert tb % TOKENS_PER_ITER == 0 and seq % tb == 0
    rows_spec = pl.BlockSpec((tb, N_PICKS), lambda i: (i, 0), memory_space=pltpu.SMEM)
    picks_spec = pl.BlockSpec((tb, N_PICKS), lambda i: (i, 0))
    tok_spec = pl.BlockSpec((tb, D), lambda i: (i, 0))
    table_spec = pl.BlockSpec((u_experts.shape[0] * SLAB_ROWS, LANES), lambda i: (0, 0),
                              pipeline_mode=pl.Buffered(1))
    wide = N_PICKS * FEATURE_ROWS
    z = pl.pallas_call(
        _peer_u_kernel,
        grid=(T // tb,),
        in_specs=[rows_spec, tok_spec, table_spec],
        out_specs=picks_spec,
        out_shape=jax.ShapeDtypeStruct((T, N_PICKS), F32),
        scratch_shapes=[pltpu.VMEM((tb, wide), F32)],
        compiler_params=_cparams(("arbitrary",)),
        name="peer_u",
    )(rows, h2, _packed_expert_table(u_experts))
    return pl.pallas_call(
        _peer_v_kernel,
        grid=(T // tb,),
        in_specs=[rows_spec, picks_spec, picks_spec, table_spec, tok_spec,
                  pl.BlockSpec((1,) + mods.shape[1:], lambda i: ((i * tb) // seq, 0, 0)),
                  pl.BlockSpec((1, D), lambda i: (0, 0))],
        out_specs=tok_spec,
        out_shape=jax.ShapeDtypeStruct((T, D), F32),
        scratch_shapes=[pltpu.VMEM((tb, wide), F32)],
        compiler_params=_cparams(("arbitrary",)),
        name="peer_v",
    )(rows, z, gate, _packed_expert_table(v_experts), x1, mods, final_g)


def _token_tile(S):
    return min(512, S)


def kernel(x, c, w_ada, b_ada, norm1_g, w_in, sink_a, t5_table, rpb_b, out_norm_a, out_norm_b,
           w_out, norm2_g, w_query, sub_keys, u_experts, v_experts, final_g):
    B, S, D = x.shape
    assert w_ada.shape[0] == 1, "the last residual add is fused with the final norm: one layer only"
    tm = _token_tile(S)
    mods = _ada(c, w_ada[0], b_ada[0])
    qa, ka, va, qb, kb, vb = _in_proj(x, mods, norm1_g[0].reshape(1, D), w_in[0], tm)
    o_a = _win_attn(qa, ka, va, sink_a[0], t5_table)
    o_b = _nbr_attn(qb, kb, vb, rpb_b[0])
    x1, h2 = _out_proj(o_a, o_b, x, mods, out_norm_a[0].reshape(1, W_QA), out_norm_b[0].reshape(1, W_B),
                       w_out[0], norm2_g[0].reshape(1, D), tm)
    h2f = h2.reshape(B * S, D)
    rows, gate = _route(h2f, w_query[0], sub_keys[0], min(256, B * S))
    out = _peer_and_final(h2f, rows, gate, u_experts[0], v_experts[0], x1.reshape(B * S, D), mods,
                          final_g.reshape(1, D), S, min(128, S))
    return out.reshape(B, S, D)
```

```python
import functools
import math

import numpy as np
import jax
import jax.numpy as jnp
from jax import lax
from jax.experimental import pallas as pl
from jax.experimental.pallas import tpu as pltpu

F32 = jnp.float32
BF16 = jnp.bfloat16

HEAD_DIM = 64
N_HEADS_A = 8
N_KV_A = 2
N_HEADS_B = 8
WINDOW = 128
WIN_BLOCK = 128
T5_BUCKETS = 32
T5_MAX_DIST = 128
GRID_W = 64
NA_ROWS = 8
NA_COLS = 16
PEER_HEADS = 8
PEER_KEYS = 128
PEER_QDIM = 256
PEER_HALF = 128
PEER_TOPK = 16
N_PICKS = PEER_HEADS * PEER_TOPK
EPS = 1e-6
NEG = -1e30

W_QA = N_HEADS_A * HEAD_DIM
W_KVA = N_KV_A * HEAD_DIM
W_B = N_HEADS_B * HEAD_DIM

LANES = 128
VMEM_LIMIT = 60 * 1024 * 1024


def _cparams(sem):
    return pltpu.CompilerParams(dimension_semantics=sem, vmem_limit_bytes=VMEM_LIMIT)


def _ada_kernel(c_ref, w_ref, b_ref, o_ref):
    c = c_ref[...]
    act = c * jax.nn.sigmoid(c)
    o_ref[...] = jnp.dot(act, w_ref[...], preferred_element_type=F32,
                         precision=lax.Precision.HIGHEST) + b_ref[...]


def _ada(c, w_ada, b_ada):
    B, D = c.shape
    n_out = w_ada.shape[1]
    rows = 8
    c_pad = jnp.zeros((rows, D), F32).at[:B].set(c)
    out = pl.pallas_call(
        _ada_kernel,
        grid=(n_out // D,),
        in_specs=[pl.BlockSpec((rows, D), lambda j: (0, 0)),
                  pl.BlockSpec((D, D), lambda j: (0, j)),
                  pl.BlockSpec((1, D), lambda j: (0, j))],
        out_specs=pl.BlockSpec((rows, D), lambda j: (0, j)),
        out_shape=jax.ShapeDtypeStruct((rows, n_out), F32),
        compiler_params=_cparams(("arbitrary",)),
        name="ada",
    )(c_pad, w_ada, b_ada.reshape(1, n_out))
    return out[:B].reshape(B, n_out // D, D)


def _rms(x, g):
    return x * lax.rsqrt(jnp.mean(x * x, axis=-1, keepdims=True) + EPS) * g


def _in_proj_kernel(x_ref, mod_ref, g_ref, wq_ref, wk_ref, wv_ref, wqb_ref, wkb_ref, wvb_ref,
                    qa_ref, ka_ref, va_ref, qb_ref, kb_ref, vb_ref):
    x = x_ref[0]
    h = _rms(x, g_ref[...]) * (1.0 + mod_ref[0, 1:2, :]) + mod_ref[0, 0:1, :]
    hb = h.astype(BF16)
    scale = HEAD_DIM ** -0.5
    qa_ref[0] = (jnp.dot(hb, wq_ref[...], preferred_element_type=F32) * scale).astype(BF16)
    ka_ref[0] = jnp.dot(hb, wk_ref[...], preferred_element_type=F32).astype(BF16)
    va_ref[0] = jnp.dot(hb, wv_ref[...], preferred_element_type=F32).astype(BF16)
    qb_ref[0] = (jnp.dot(hb, wqb_ref[...], preferred_element_type=F32) * scale).astype(BF16)
    kb_ref[0] = jnp.dot(hb, wkb_ref[...], preferred_element_type=F32).astype(BF16)
    vb_ref[0] = jnp.dot(hb, wvb_ref[...], preferred_element_type=F32).astype(BF16)


def _dup_kv_cols(w):
    a, b = w[:, :HEAD_DIM], w[:, HEAD_DIM:]
    return jnp.concatenate([a, a, b, b], axis=1)


def _in_proj(x, mods, g1, w_in, tm):
    B, S, D = x.shape
    wb = w_in.astype(BF16)
    o = 0
    wq = wb[:, o:o + W_QA]; o += W_QA
    wk = _dup_kv_cols(wb[:, o:o + W_KVA]); o += W_KVA
    wv = _dup_kv_cols(wb[:, o:o + W_KVA]); o += W_KVA
    wqb = wb[:, o:o + W_B]; o += W_B
    wkb = wb[:, o:o + W_B]; o += W_B
    wvb = wb[:, o:o + W_B]
    full = lambda a: pl.BlockSpec(a.shape, lambda b, i: (0, 0))
    tok = lambda w: pl.BlockSpec((1, tm, w), lambda b, i: (b, i, 0))
    widths = (W_QA, 2 * W_KVA, 2 * W_KVA, W_B, W_B, W_B)
    return pl.pallas_call(
        _in_proj_kernel,
        grid=(B, S // tm),
        in_specs=[tok(D), pl.BlockSpec((1, 6, D), lambda b, i: (b, 0, 0)), full(g1),
                  full(wq), full(wk), full(wv), full(wqb), full(wkb), full(wvb)],
        out_specs=[tok(w) for w in widths],
        out_shape=[jax.ShapeDtypeStruct((B, S, w), BF16) for w in widths],
        compiler_params=_cparams(("parallel", "parallel")),
        name="in_proj",
    )(x, mods, g1, wq, wk, wv, wqb, wkb, wvb)


def _t5_bucket_np(rel):
    half = T5_BUCKETS // 2
    max_exact = half // 2
    ret = np.where(rel > 0, half, 0)
    n = np.abs(rel)
    nf = np.maximum(n, 1).astype(np.float32)
    large = max_exact + (np.log(nf / np.float32(max_exact)) / np.float32(math.log(T5_MAX_DIST / max_exact))
                         * np.float32(half - max_exact)).astype(np.int32)
    large = np.minimum(large, half - 1)
    return ret + np.where(n < max_exact, n, large)


def _lookup_kernel(tab_ref, sel_ref, o_ref):
    sel = sel_ref[...]
    onehot = (lax.broadcasted_iota(jnp.int32, (tab_ref.shape[1], sel.shape[1]), 0) == sel).astype(F32)
    val = jnp.dot(tab_ref[...], onehot, preferred_element_type=F32, precision=lax.Precision.HIGHEST)
    o_ref[...] = jnp.where(sel >= 0, val, NEG)


def _lookup(table, sel):
    R, M = table.shape
    m_pad = -(-M // 8) * 8
    table = jnp.zeros((R, m_pad), F32).at[:, :M].set(table.astype(F32))
    n = sel.shape[0]
    tn = min(n, 4096)
    return pl.pallas_call(
        _lookup_kernel,
        grid=(n // tn,),
        in_specs=[pl.BlockSpec((R, m_pad), lambda j: (0, 0)), pl.BlockSpec((1, tn), lambda j: (0, j))],
        out_specs=pl.BlockSpec((R, tn), lambda j: (0, j)),
        out_shape=jax.ShapeDtypeStruct((R, n), F32),
        compiler_params=_cparams(("parallel",)),
        name="bias_lookup",
    )(table, jnp.asarray(sel.reshape(1, n), jnp.int32))


def _softmax_rows(s, extra=None):
    m = jnp.max(s, axis=-1, keepdims=True)
    if extra is not None:
        m = jnp.maximum(m, extra)
    e = jnp.exp(s - m)
    den = jnp.sum(e, axis=-1, keepdims=True)
    if extra is not None:
        den = den + jnp.exp(extra - m)
    return e / den


WIN_BLOCKS_PER_STEP = 4


def _win_attn_kernel(sink_ref, q_ref, kp_ref, kc_ref, kn_ref, vp_ref, vc_ref, vn_ref, bias_ref, o_ref,
                     s_scr, p_scr, *, seq):
    step = pl.program_id(1)
    k = jnp.concatenate([kp_ref[0], kc_ref[0], kn_ref[0]], axis=0)
    v = jnp.concatenate([vp_ref[0], vc_ref[0], vn_ref[0]], axis=0)
    span = WIN_BLOCK + 2 * WINDOW
    col = lax.broadcasted_iota(jnp.int32, (WIN_BLOCK, span), 1)
    low = lax.broadcasted_iota(jnp.int32, (WIN_BLOCK, LANES), 1) < HEAD_DIM
    zero = jnp.zeros((WIN_BLOCK, LANES), BF16)
    group = lambda a, blk, h: a[blk * WIN_BLOCK:blk * WIN_BLOCK + span,
                                (h // (N_HEADS_A // N_KV_A)) * LANES:(h // (N_HEADS_A // N_KV_A) + 1) * LANES]
    for blk in range(WIN_BLOCKS_PER_STEP):
        rows = slice(blk * WIN_BLOCK, (blk + 1) * WIN_BLOCK)
        kpos = col + (step * WIN_BLOCKS_PER_STEP + blk - 1) * WIN_BLOCK
        valid = (kpos >= 0) & (kpos < seq)
        for h in range(N_HEADS_A):
            qp = q_ref[0, rows, (h // 2) * LANES:(h // 2 + 1) * LANES]
            qh = jnp.where(low if h % 2 == 0 else jnp.logical_not(low), qp, zero)
            s_scr[h] = lax.dot_general(qh, group(k, blk, h), (((1,), (1,)), ((), ())),
                                       preferred_element_type=F32)
        for h in range(N_HEADS_A):
            s = jnp.where(valid, s_scr[h] + bias_ref[h], NEG)
            p_scr[h] = _softmax_rows(s, extra=sink_ref[h]).astype(BF16)
        for pair in range(N_HEADS_A // 2):
            outs = [jnp.dot(p_scr[2 * pair + par], group(v, blk, 2 * pair + par), preferred_element_type=F32)
                    for par in range(2)]
            o_ref[0, rows, pair * LANES:(pair + 1) * LANES] = jnp.where(low, outs[0], outs[1])


def _win_attn(qa, ka, va, sink, t5_table):
    B, S, _ = qa.shape
    nb = S // WIN_BLOCK
    per = WIN_BLOCKS_PER_STEP
    assert nb % per == 0
    span = WIN_BLOCK + 2 * WINDOW
    rel = np.arange(span)[None, :] - WINDOW - np.arange(WIN_BLOCK)[:, None]
    bucket = np.where(np.abs(rel) <= WINDOW, _t5_bucket_np(rel), -1)
    bias = _lookup(t5_table.T, bucket.reshape(-1)).reshape(N_HEADS_A, WIN_BLOCK, span)
    qspec = pl.BlockSpec((1, per * WIN_BLOCK, W_QA), lambda b, i: (b, i, 0))
    cur = pl.BlockSpec((1, per * WIN_BLOCK, 2 * W_KVA), lambda b, i: (b, i, 0))
    edge = lambda off: pl.BlockSpec((1, WIN_BLOCK, 2 * W_KVA),
                                    lambda b, i: (b, jnp.clip(i * per + off, 0, nb - 1), 0))
    return pl.pallas_call(
        functools.partial(_win_attn_kernel, seq=S),
        grid=(B, nb // per),
        in_specs=[pl.BlockSpec(memory_space=pltpu.SMEM), qspec,
                  edge(-1), cur, edge(per), edge(-1), cur, edge(per),
                  pl.BlockSpec(bias.shape, lambda b, i: (0, 0, 0))],
        out_specs=qspec,
        out_shape=jax.ShapeDtypeStruct((B, S, W_QA), F32),
        scratch_shapes=[pltpu.VMEM((N_HEADS_A, WIN_BLOCK, span), F32),
                        pltpu.VMEM((N_HEADS_A, WIN_BLOCK, span), BF16)],
        compiler_params=_cparams(("parallel", "parallel")),
        name="win_attn",
    )(sink.astype(F32), qa, ka, ka, ka, va, va, va, bias)


NBR_ROWS_PER_STEP = 8


def _nbr_attn_kernel(q_ref, kp_ref, kc_ref, kn_ref, vp_ref, vc_ref, vn_ref, bias_ref, o_ref,
                     k_scr, v_scr, s_scr, p_scr, *, n_rows):
    step = pl.program_id(1)
    blk = NBR_ROWS_PER_STEP * GRID_W
    for j, (kr, vr) in enumerate(((kp_ref, vp_ref), (kc_ref, vc_ref), (kn_ref, vn_ref))):
        k_scr[j * blk:(j + 1) * blk, :] = kr[0]
        v_scr[j * blk:(j + 1) * blk, :] = vr[0]
    low = lax.broadcasted_iota(jnp.int32, (GRID_W, LANES), 1) < HEAD_DIM
    zero = jnp.zeros((GRID_W, LANES), BF16)
    band = NA_ROWS * GRID_W

    def one_row(i, carry):
        r = step * NBR_ROWS_PER_STEP + i
        rs = jnp.clip(r - NA_ROWS // 2, 0, n_rows - NA_ROWS)
        off = pl.multiple_of((rs - (step - 1) * NBR_ROWS_PER_STEP) * GRID_W, GRID_W)
        d = r - rs
        q_rows = pl.ds(pl.multiple_of(i * GRID_W, GRID_W), GRID_W)
        pair_lanes = lambda h: slice((h // 2) * LANES, (h // 2 + 1) * LANES)
        for h in range(N_HEADS_B):
            qp = q_ref[0, q_rows, pair_lanes(h)]
            qh = jnp.where(low if h % 2 == 0 else jnp.logical_not(low), qp, zero)
            s_scr[h] = lax.dot_general(qh, k_scr[pl.ds(off, band), pair_lanes(h)], (((1,), (1,)), ((), ())),
                                       preferred_element_type=F32)
        for h in range(N_HEADS_B):
            p_scr[h] = _softmax_rows(s_scr[h] + bias_ref[d, h]).astype(BF16)
        for pair in range(N_HEADS_B // 2):
            sl = slice(pair * LANES, (pair + 1) * LANES)
            vp = v_scr[pl.ds(off, band), sl]
            outs = [jnp.dot(p_scr[2 * pair + par], vp, preferred_element_type=F32) for par in range(2)]
            o_ref[0, q_rows, sl] = jnp.where(low, outs[0], outs[1])
        return carry

    lax.fori_loop(0, NBR_ROWS_PER_STEP, one_row, 0)


def _nbr_bias(rpb):
    cols = np.arange(GRID_W)
    col_start = np.clip(cols - NA_COLS // 2, 0, GRID_W - NA_COLS)
    kc = np.arange(GRID_W)
    inside = (kc[None, :] >= col_start[:, None]) & (kc[None, :] < col_start[:, None] + NA_COLS)
    col_off = np.where(inside, kc[None, :] - cols[:, None] + (NA_COLS - 1), -1)
    table = jnp.stack([rpb[:, NA_ROWS - 1 - d:2 * NA_ROWS - 1 - d, :] for d in range(NA_ROWS)])
    b = _lookup(table.reshape(NA_ROWS * N_HEADS_B * NA_ROWS, 2 * NA_COLS - 1), col_off.reshape(-1))
    b = b.reshape(NA_ROWS, N_HEADS_B, NA_ROWS, GRID_W, GRID_W).transpose(0, 1, 3, 2, 4)
    return b.reshape(NA_ROWS, N_HEADS_B, GRID_W, NA_ROWS * GRID_W)


def _nbr_attn(qb, kb, vb, rpb):
    B, S, _ = qb.shape
    rows = S // GRID_W
    assert rows >= NA_ROWS
    assert rows % NBR_ROWS_PER_STEP == 0 and NBR_ROWS_PER_STEP >= NA_ROWS
    bias = _nbr_bias(rpb)
    n_steps = rows // NBR_ROWS_PER_STEP
    blk = NBR_ROWS_PER_STEP * GRID_W
    spec = lambda off: pl.BlockSpec((1, blk, W_B), lambda b, i: (b, jnp.clip(i + off, 0, n_steps - 1), 0))
    return pl.pallas_call(
        functools.partial(_nbr_attn_kernel, n_rows=rows),
        grid=(B, n_steps),
        in_specs=[spec(0), spec(-1), spec(0), spec(1), spec(-1), spec(0), spec(1),
                  pl.BlockSpec(bias.shape, lambda b, i: (0, 0, 0, 0), pipeline_mode=pl.Buffered(1))],
        out_specs=spec(0),
        out_shape=jax.ShapeDtypeStruct((B, S, W_B), F32),
        scratch_shapes=[pltpu.VMEM((3 * blk, W_B), BF16)] * 2
                       + [pltpu.VMEM((N_HEADS_B, GRID_W, NA_ROWS * GRID_W), F32),
                          pltpu.VMEM((N_HEADS_B, GRID_W, NA_ROWS * GRID_W), BF16)],
        compiler_params=_cparams(("parallel", "parallel")),
        name="nbr_attn",
    )(qb, kb, kb, kb, vb, vb, vb, bias)


def _out_proj_kernel(oa_ref, ob_ref, x_ref, mod_ref, ga_ref, gb_ref, wa_ref, wb_ref, g2_ref, x1_ref, h2_ref):
    na = _rms(oa_ref[0], ga_ref[...]).astype(BF16)
    nb = _rms(ob_ref[0], gb_ref[...]).astype(BF16)
    y = (jnp.dot(na, wa_ref[...], preferred_element_type=F32)
         + jnp.dot(nb, wb_ref[...], preferred_element_type=F32))
    x1 = x_ref[0] + mod_ref[0, 2:3, :] * y
    x1_ref[0] = x1
    h2_ref[0] = _rms(x1, g2_ref[...]) * (1.0 + mod_ref[0, 4:5, :]) + mod_ref[0, 3:4, :]


def _out_proj(o_a, o_b, x, mods, ga, gb, w_out, g2, tm):
    B, S, D = x.shape
    wb16 = w_out.astype(BF16)
    wa, wb = wb16[:W_QA], wb16[W_QA:]
    full = lambda a: pl.BlockSpec(a.shape, lambda b, i: (0, 0))
    tok = lambda w: pl.BlockSpec((1, tm, w), lambda b, i: (b, i, 0))
    return pl.pallas_call(
        _out_proj_kernel,
        grid=(B, S // tm),
        in_specs=[tok(W_QA), tok(W_B), tok(D), pl.BlockSpec((1, 6, D), lambda b, i: (b, 0, 0)),
                  full(ga), full(gb), full(wa), full(wb), full(g2)],
        out_specs=[tok(D), tok(D)],
        out_shape=[jax.ShapeDtypeStruct((B, S, D), F32)] * 2,
        compiler_params=_cparams(("parallel", "parallel")),
        name="out_proj",
    )(o_a, o_b, x, mods, ga, gb, wa, wb, g2)


def _topk_rows(s, k, payload=None):
    n_rows = s.shape[0]
    row = lax.broadcasted_iota(jnp.int32, s.shape, 0)
    vals, picked = [], []
    for _ in range(k):
        m = jnp.max(s, axis=0, keepdims=True)
        am = jnp.min(jnp.where(s == m, row, n_rows), axis=0, keepdims=True)
        hit = row == am
        vals.append(m)
        picked.append(am if payload is None else jnp.sum(jnp.where(hit, payload, 0), axis=0, keepdims=True))
        s = jnp.where(hit, -jnp.inf, s)
    return jnp.concatenate(vals, axis=0), jnp.concatenate(picked, axis=0)


def _product_candidates(v1, i1, v2, i2):
    assert PEER_TOPK == 16
    n = v1.shape[1]
    vals, ids = [], []
    for a in range(8):
        nb = PEER_TOPK if a == 0 else 8
        sv = v1[a:a + 1, :] + v2[:nb, :]
        limit = PEER_TOPK // (a + 1)
        if limit < nb:
            sv = jnp.where(lax.broadcasted_iota(jnp.int32, (nb, n), 0) < limit, sv, -jnp.inf)
        vals.append(sv)
        ids.append(i1[a:a + 1, :] * PEER_KEYS + i2[:nb, :])
    vals.append(v1[8:, :] + v2[0:1, :])
    ids.append(i1[8:, :] * PEER_KEYS + i2[0:1, :])
    return jnp.concatenate(vals, axis=0), jnp.concatenate(ids, axis=0)


def _route_kernel(h_ref, wq_ref, keys_ref, idx_ref, gate_ref):
    hb = h_ref[...].astype(BF16)
    idx_rows, gate_rows = [], []
    for h in range(PEER_HEADS):
        tops = []
        for side in range(2):
            c0 = h * PEER_QDIM + side * PEER_HALF
            qh = jnp.dot(hb, wq_ref[:, c0:c0 + PEER_HALF], preferred_element_type=F32).astype(BF16)
            s = lax.dot_general(keys_ref[h, side], qh, (((1,), (1,)), ((), ())),
                                preferred_element_type=F32)
            tops.append(_topk_rows(s, PEER_TOPK))
        (v1, i1), (v2, i2) = tops
        cand, cand_ids = _product_candidates(v1, i1, v2, i2)
        vs, ids = _topk_rows(cand, PEER_TOPK, payload=cand_ids)
        idx_rows.append(ids)
        ex = jnp.exp(vs - jnp.max(vs, axis=0, keepdims=True))
        gate_rows.append(ex / jnp.sum(ex, axis=0, keepdims=True))
    idx_ref[...] = (jnp.concatenate(idx_rows, axis=0) * SLAB_ROWS).T
    gate_ref[...] = jnp.concatenate(gate_rows, axis=0).T


def _route(h2, w_query, sub_keys, tm):
    T, D = h2.shape
    wq = w_query.astype(BF16)
    keys = sub_keys.astype(BF16)
    return pl.pallas_call(
        _route_kernel,
        grid=(T // tm,),
        in_specs=[pl.BlockSpec((tm, D), lambda i: (i, 0)),
                  pl.BlockSpec(wq.shape, lambda i: (0, 0)),
                  pl.BlockSpec(keys.shape, lambda i: (0, 0, 0, 0))],
        out_specs=[pl.BlockSpec((tm, N_PICKS), lambda i: (i, 0))] * 2,
        out_shape=[jax.ShapeDtypeStruct((T, N_PICKS), jnp.int32),
                   jax.ShapeDtypeStruct((T, N_PICKS), F32)],
        compiler_params=_cparams(("parallel",)),
        name="route",
    )(h2, wq, keys)


SLAB_ROWS = 4
FEATURE_ROWS = 8
TOKENS_PER_ITER = 32


def _pack_rows_kernel(x_ref, o_ref):
    o_ref[...] = pltpu.bitcast(x_ref[...], jnp.int32)


def _packed_expert_table(experts):
    n_experts, D = experts.shape
    assert D == FEATURE_ROWS * LANES
    rows = experts.astype(BF16).reshape(n_experts * FEATURE_ROWS, LANES)
    tile = 2048 * FEATURE_ROWS
    return pl.pallas_call(
        _pack_rows_kernel,
        grid=(rows.shape[0] // tile,),
        in_specs=[pl.BlockSpec((tile, LANES), lambda i: (i, 0))],
        out_specs=pl.BlockSpec((tile // 2, LANES), lambda i: (i, 0)),
        out_shape=jax.ShapeDtypeStruct((rows.shape[0] // 2, LANES), jnp.int32),
        compiler_params=_cparams(("parallel",)),
        name="pack_rows",
    )(rows)


def _token_expert_rows(tbl_ref, picks):
    pairs = []
    for m in range(N_PICKS // 2):
        words = jnp.concatenate(
            [tbl_ref[pl.ds(pl.multiple_of(picks[2 * m + j], SLAB_ROWS), SLAB_ROWS), :] for j in range(2)], axis=0)
        pairs.append(pltpu.bitcast(words, BF16))
    return jnp.concatenate(pairs, axis=0)


def _own_row_mask():
    shape = (FEATURE_ROWS, N_PICKS * FEATURE_ROWS)
    return lax.broadcasted_iota(jnp.int32, shape, 1) % FEATURE_ROWS == lax.broadcasted_iota(jnp.int32, shape, 0)


def _token_loop(n_tokens, one_token):
    def body(i, carry):
        for u in range(TOKENS_PER_ITER):
            one_token(i * TOKENS_PER_ITER + u)
        return carry

    lax.fori_loop(0, n_tokens // TOKENS_PER_ITER, body, 0)


def _peer_u_kernel(rows_ref, x_ref, tbl_ref, z_ref, y_scr):
    own = _own_row_mask()

    def token(t):
        rows = _token_expert_rows(tbl_ref, rows_ref.at[t])
        x_row = x_ref[pl.ds(t, 1), :].astype(BF16)
        x_rows = jnp.concatenate([x_row[:, r * LANES:(r + 1) * LANES] for r in range(FEATURE_ROWS)], axis=0)
        part = lax.dot_general(x_rows, rows, (((1,), (1,)), ((), ())),
                               preferred_element_type=F32)
        y_scr[pl.ds(t, 1), :] = jnp.sum(jnp.where(own, part, 0.0), axis=0, keepdims=True)

    _token_loop(x_ref.shape[0], token)
    shape = (N_PICKS * FEATURE_ROWS, N_PICKS)
    group = (lax.broadcasted_iota(jnp.int32, shape, 0) // FEATURE_ROWS
             == lax.broadcasted_iota(jnp.int32, shape, 1)).astype(F32)
    z_ref[...] = jnp.dot(y_scr[...], group, preferred_element_type=F32, precision=lax.Precision.HIGHEST)


def _gelu_tanh(z):
    return 0.5 * z * (1.0 + jnp.tanh(math.sqrt(2.0 / math.pi) * (z + 0.044715 * (z * z * z))))


def _peer_v_kernel(rows_ref, z_ref, gate_ref, tbl_ref, x1_ref, mod_ref, g_ref, o_ref, w_scr):
    a = (_gelu_tanh(z_ref[...]) * gate_ref[...]).astype(BF16)
    shape = (N_PICKS, N_PICKS * FEATURE_ROWS)
    spread = (lax.broadcasted_iota(jnp.int32, shape, 1) // FEATURE_ROWS
              == lax.broadcasted_iota(jnp.int32, shape, 0)).astype(BF16)
    w_scr[...] = jnp.dot(a, spread, preferred_element_type=F32)
    own = _own_row_mask()

    def token(t):
        rows = _token_expert_rows(tbl_ref, rows_ref.at[t])
        left = jnp.where(own, w_scr[pl.ds(t, 1), :], 0.0).astype(BF16)
        feat = jnp.dot(left, rows, preferred_element_type=F32)
        o_ref[pl.ds(t, 1), :] = jnp.concatenate([feat[r:r + 1, :] for r in range(FEATURE_ROWS)], axis=1)

    _token_loop(o_ref.shape[0], token)
    o_ref[...] = _rms(x1_ref[...] + mod_ref[0, 5:6, :] * o_ref[...], g_ref[...])


def _peer_and_final(h2, rows, gate, u_experts, v_experts, x1, mods, final_g, seq, tb):
    T, D = h2.shape
    assert tb % TOKENS_PER_ITER == 0 and seq % tb == 0
    rows_spec = pl.BlockSpec((tb, N_PICKS), lambda i: (i, 0), memory_space=pltpu.SMEM)
    picks_spec = pl.BlockSpec((tb, N_PICKS), lambda i: (i, 0))
    tok_spec = pl.BlockSpec((tb, D), lambda i: (i, 0))
    table_spec = pl.BlockSpec((u_experts.shape[0] * SLAB_ROWS, LANES), lambda i: (0, 0),
                              pipeline_mode=pl.Buffered(1))
    wide = N_PICKS * FEATURE_ROWS
    z = pl.pallas_call(
        _peer_u_kernel,
        grid=(T // tb,),
        in_specs=[rows_spec, tok_spec, table_spec],
        out_specs=picks_spec,
        out_shape=jax.ShapeDtypeStruct((T, N_PICKS), F32),
        scratch_shapes=[pltpu.VMEM((tb, wide), F32)],
        compiler_params=_cparams(("arbitrary",)),
        name="peer_u",
    )(rows, h2, _packed_expert_table(u_experts))
    return pl.pallas_call(
        _peer_v_kernel,
        grid=(T // tb,),
        in_specs=[rows_spec, picks_spec, picks_spec, table_spec, tok_spec,
                  pl.BlockSpec((1,) + mods.shape[1:], lambda i: ((i * tb) // seq, 0, 0)),
                  pl.BlockSpec((1, D), lambda i: (0, 0))],
        out_specs=tok_spec,
        out_shape=jax.ShapeDtypeStruct((T, D), F32),
        scratch_shapes=[pltpu.VMEM((tb, wide), F32)],
        compiler_params=_cparams(("arbitrary",)),
        name="peer_v",
    )(rows, z, gate, _packed_expert_table(v_experts), x1, mods, final_g)


def _token_tile(S):
    return min(512, S)


def kernel(x, c, w_ada, b_ada, norm1_g, w_in, sink_a, t5_table, rpb_b, out_norm_a, out_norm_b,
           w_out, norm2_g, w_query, sub_keys, u_experts, v_experts, final_g):
    B, S, D = x.shape
    assert w_ada.shape[0] == 1, "the last residual add is fused with the final norm: one layer only"
    tm = _token_tile(S)
    mods = _ada(c, w_ada[0], b_ada[0])
    qa, ka, va, qb, kb, vb = _in_proj(x, mods, norm1_g[0].reshape(1, D), w_in[0], tm)
    o_a = _win_attn(qa, ka, va, sink_a[0], t5_table)
    o_b = _nbr_attn(qb, kb, vb, rpb_b[0])
    x1, h2 = _out_proj(o_a, o_b, x, mods, out_norm_a[0].reshape(1, W_QA), out_norm_b[0].reshape(1, W_B),
                       w_out[0], norm2_g[0].reshape(1, D), tm)
    h2f = h2.reshape(B * S, D)
    rows, gate = _route(h2f, w_query[0], sub_keys[0], min(256, B * S))
    out = _peer_and_final(h2f, rows, gate, u_experts[0], v_experts[0], x1.reshape(B * S, D), mods,
                          final_g.reshape(1, D), S, min(128, S))
    return out.reshape(B, S, D)
```
